```python
import jax, jax.numpy as jnp
from jax import lax
import numpy as np

D_MODEL = 1024
BATCH = 2
SEQ = 8192
DEPTH = 1

CHUNK = 64
MIX_WIDTH = D_MODEL
A_WIDTH = MIX_WIDTH // 2
B_WIDTH = MIX_WIDTH - A_WIDTH
GMLP_BLOCK = 128
A_HEADS = 4
A_HEAD_DIM = A_WIDTH // A_HEADS
CONV_WIDTH = 31
B_GROUPS = 8
FFN_HIDDEN = ((8 * D_MODEL + 3 * 256 - 1) // (3 * 256)) * 256
IN_WIDTH = 2 * A_WIDTH + 2 * B_WIDTH
RMS_EPS = 1e-6
LN_EPS = 1e-5

kernel_name = "hybrid_gmlp_conformer_conv_block"


def rms_norm(x, g):
    x32 = x.astype(jnp.float32)
    y = x32 * lax.rsqrt(jnp.mean(x32 * x32, axis=-1, keepdims=True) + RMS_EPS)
    return (y * g.astype(jnp.float32)).astype(x.dtype)


def layer_norm(x, g, b):
    x32 = x.astype(jnp.float32)
    mu = jnp.mean(x32, axis=-1, keepdims=True)
    xc = x32 - mu
    var = jnp.mean(xc * xc, axis=-1, keepdims=True)
    y = xc * lax.rsqrt(var + LN_EPS)
    return (y * g.astype(jnp.float32) + b.astype(jnp.float32)).astype(x.dtype)


def gmlp_spatial_gating(z, ln_g, ln_b, w_s, b_s):
    z = jax.nn.gelu(z)
    u, v = jnp.split(z, 2, axis=-1)
    v = layer_norm(v, ln_g, ln_b)
    bsz, s, _ = v.shape
    vb = v.reshape(bsz, s // GMLP_BLOCK, GMLP_BLOCK, A_HEADS, A_HEAD_DIM)
    chunk_id = jnp.arange(GMLP_BLOCK) // CHUNK
    mask = chunk_id[None, :] <= chunk_id[:, None]
    w = jnp.where(mask[None], w_s, jnp.zeros((), w_s.dtype))
    sg = jnp.einsum('hij,bnjhc->bnihc', w, vb) + b_s.T[None, None, :, :, None]
    return u * sg.reshape(bsz, s, A_WIDTH)


def conformer_conv_module(z, conv_w, conv_b, ln_g, ln_b):
    a, g = jnp.split(z, 2, axis=-1)
    h = a * jax.nn.sigmoid(g)
    h = lax.conv_general_dilated(
        h, conv_w[:, None, :].astype(h.dtype),
        window_strides=(1,), padding=((CONV_WIDTH - 1, 0),),
        dimension_numbers=('NWC', 'WIO', 'NWC'),
        feature_group_count=B_WIDTH) + conv_b
    h = layer_norm(h, ln_g, ln_b)
    return jax.nn.silu(h)


def swiglu(x, w_gate, w_up, w_down):
    return jnp.einsum('bsf,fd->bsd', jax.nn.silu(jnp.einsum('bsd,df->bsf', x, w_gate)) * jnp.einsum('bsd,df->bsf', x, w_up), w_down)


def setup_inputs(seed: int = 0) -> dict:
    key = jax.random.key(seed)
    ks = jax.random.split(key, 20)
    f32 = jnp.float32
    d = D_MODEL
    inp = {}
    inp['x'] = jax.random.normal(ks[0], (BATCH, SEQ, d), f32)
    inp['norm1_g'] = 1.0 + 0.05 * jax.random.normal(ks[1], (d,), f32)
    inp['w_in'] = jax.random.normal(ks[2], (d, IN_WIDTH), f32) * d ** -0.5
    inp['gmlp_ln_g'] = 1.0 + 0.05 * jax.random.normal(ks[3], (A_WIDTH,), f32)
    inp['gmlp_ln_b'] = 0.02 * jax.random.normal(ks[4], (A_WIDTH,), f32)
    inp['gmlp_w_s'] = jax.random.normal(ks[5], (A_HEADS, GMLP_BLOCK, GMLP_BLOCK), f32) * GMLP_BLOCK ** -0.5
    inp['gmlp_b_s'] = 1.0 + 0.1 * jax.random.normal(ks[6], (A_HEADS, GMLP_BLOCK), f32)
    inp['conv_w'] = jax.random.normal(ks[7], (CONV_WIDTH, B_WIDTH), f32) * CONV_WIDTH ** -0.5
    inp['conv_b'] = 0.02 * jax.random.normal(ks[8], (B_WIDTH,), f32)
    inp['conv_ln_g'] = 1.0 + 0.05 * jax.random.normal(ks[9], (B_WIDTH,), f32)
    inp['conv_ln_b'] = 0.02 * jax.random.normal(ks[10], (B_WIDTH,), f32)
    inp['w_out'] = jax.random.normal(ks[11], (MIX_WIDTH, d), f32) * MIX_WIDTH ** -0.5
    inp['norm2_g'] = 1.0 + 0.05 * jax.random.normal(ks[12], (d,), f32)
    inp['w_gate'] = jax.random.normal(ks[13], (d, FFN_HIDDEN), f32) * d ** -0.5
    inp['w_up'] = jax.random.normal(ks[14], (d, FFN_HIDDEN), f32) * d ** -0.5
    inp['w_down'] = jax.random.normal(ks[15], (FFN_HIDDEN, d), f32) * FFN_HIDDEN ** -0.5
    inp['final_norm_g'] = 1.0 + 0.05 * jax.random.normal(ks[16], (d,), f32)
    return inp


def reference(x, norm1_g, w_in, gmlp_ln_g, gmlp_ln_b, gmlp_w_s, gmlp_b_s, conv_w, conv_b,
              conv_ln_g, conv_ln_b, w_out, norm2_g, w_gate, w_up, w_down, final_norm_g):
    h = x
    for _ in range(DEPTH):
        xn = rms_norm(h, norm1_g)
        z = jnp.einsum('bsd,de->bse', xn, w_in)
        z_a = z[..., :2 * A_WIDTH]
        z_b = z[..., 2 * A_WIDTH:]
        y_a = gmlp_spatial_gating(z_a, gmlp_ln_g, gmlp_ln_b, gmlp_w_s, gmlp_b_s)
        y_b = conformer_conv_module(z_b, conv_w, conv_b, conv_ln_g, conv_ln_b)
        y = jnp.concatenate([y_a, y_b], axis=-1)
        h = h + jnp.einsum('bse,ed->bsd', y, w_out)
        h = h + swiglu(rms_norm(h, norm2_g), w_gate, w_up, w_down)
    return rms_norm(h, final_norm_g)
```

```python
import functools

import jax
import jax.numpy as jnp
from jax import lax
from jax.experimental import pallas as pl
from jax.experimental.pallas import tpu as pltpu

D_MODEL = 1024
A_WIDTH = 512
B_WIDTH = 512
GMLP_BLOCK = 128
A_HEADS = 4
A_HEAD_DIM = A_WIDTH // A_HEADS
CHUNK = 64
CONV_WIDTH = 31
FFN_HIDDEN = 2816
RMS_EPS = 1e-6
LN_EPS = 1e-5

TM = 512
HALO = 32
CONV_ROWS = 32
VMEM_LIMIT_BYTES = 58 * 1024 * 1024


def _rms(x, g):
    ms = jnp.mean(x * x, axis=-1, keepdims=True)
    return x * lax.rsqrt(ms + RMS_EPS) * g


def _ln(x, g, b):
    mu = jnp.mean(x, axis=-1, keepdims=True)
    xc = x - mu
    var = jnp.mean(xc * xc, axis=-1, keepdims=True)
    return xc * lax.rsqrt(var + LN_EPS) * g + b


def _sigmoid(x):
    return 0.5 * (jnp.tanh(0.5 * x) + 1.0)


def _block_kernel(x_ref, n1g_ref, w_in_ref, alng_ref, alnb_ref, ws_ref, bs_ref,
                  cw_ref, cb_ref, clng_ref, clnb_ref, w_out_ref, n2g_ref,
                  w_gate_ref, w_up_ref, w_down_ref, fng_ref,
                  o_ref, hbuf_ref, y_ref):
    s = pl.program_id(1)

    @pl.when(s == 0)
    def _():
        hbuf_ref[0:HALO, :] = jnp.zeros((HALO, B_WIDTH), jnp.float32)

    x = x_ref[0]
    xn = _rms(x, n1g_ref[...]).astype(jnp.bfloat16)
    z = jnp.dot(xn, w_in_ref[...], preferred_element_type=jnp.float32)

    za = jax.nn.gelu(z[:, :2 * A_WIDTH])
    u = za[:, :A_WIDTH]
    v = _ln(za[:, A_WIDTH:], alng_ref[...], alnb_ref[...]).astype(jnp.bfloat16)
    nblk = TM // GMLP_BLOCK
    pos = lax.broadcasted_iota(jnp.int32, (GMLP_BLOCK, GMLP_BLOCK), 0) // CHUNK
    posj = lax.broadcasted_iota(jnp.int32, (GMLP_BLOCK, GMLP_BLOCK), 1) // CHUNK
    mask = posj <= pos
    for h in range(A_HEADS):
        w = jnp.where(mask, ws_ref[h], 0.0).astype(jnp.bfloat16)
        c0 = h * A_HEAD_DIM
        vh = jnp.concatenate(
            [v[n * GMLP_BLOCK:(n + 1) * GMLP_BLOCK, c0:c0 + A_HEAD_DIM] for n in range(nblk)],
            axis=1)
        sg = jnp.dot(w, vh, preferred_element_type=jnp.float32)
        bias = bs_ref[:, c0:c0 + A_HEAD_DIM]
        for n in range(nblk):
            r0 = n * GMLP_BLOCK
            ya = u[r0:r0 + GMLP_BLOCK, c0:c0 + A_HEAD_DIM] * (
                sg[:, n * A_HEAD_DIM:(n + 1) * A_HEAD_DIM] + bias)
            y_ref[r0:r0 + GMLP_BLOCK, c0:c0 + A_HEAD_DIM] = ya.astype(jnp.bfloat16)

    zb = z[:, 2 * A_WIDTH:]
    hbuf_ref[HALO:HALO + TM, :] = zb[:, :B_WIDTH] * _sigmoid(zb[:, B_WIDTH:])

    def conv_step(i, carry):
        r0 = pl.multiple_of(i * CONV_ROWS, CONV_ROWS)
        win = hbuf_ref[pl.ds(r0, CONV_ROWS + HALO), :]
        acc = jnp.zeros((CONV_ROWS, B_WIDTH), jnp.float32)
        nwin = CONV_ROWS + HALO
        for b in range(8):
            rolled = win if b == 0 else pltpu.roll(win, nwin - b, axis=0)
            for a in range(5):
                d = 8 * a + b
                k = d - (HALO - (CONV_WIDTH - 1))
                if k < 0 or k >= CONV_WIDTH:
                    continue
                acc = acc + rolled[8 * a:8 * a + CONV_ROWS, :] * cw_ref[k:k + 1, :]
        acc = acc + cb_ref[...]
        hn = _ln(acc, clng_ref[...], clnb_ref[...])
        yb = hn * _sigmoid(hn)
        y_ref[pl.ds(r0, CONV_ROWS), A_WIDTH:] = yb.astype(jnp.bfloat16)
        return carry

    lax.fori_loop(0, TM // CONV_ROWS, conv_step, 0)
    hbuf_ref[0:HALO, :] = hbuf_ref[TM:TM + HALO, :]

    h1 = x + jnp.dot(y_ref[...], w_out_ref[...], preferred_element_type=jnp.float32)

    hn2 = _rms(h1, n2g_ref[...]).astype(jnp.bfloat16)
    gate = jnp.dot(hn2, w_gate_ref[...], preferred_element_type=jnp.float32)
    up = jnp.dot(hn2, w_up_ref[...], preferred_element_type=jnp.float32)
    act = (gate * _sigmoid(gate) * up).astype(jnp.bfloat16)
    h2 = h1 + jnp.dot(act, w_down_ref[...], preferred_element_type=jnp.float32)

    o_ref[0] = _rms(h2, fng_ref[...])


def _const_spec(shape):
    nd = len(shape)
    return pl.BlockSpec(shape, lambda b, s: (0,) * nd, pipeline_mode=pl.Buffered(1))


def kernel(x, norm1_g, w_in, gmlp_ln_g, gmlp_ln_b, gmlp_w_s, gmlp_b_s, conv_w, conv_b,
           conv_ln_g, conv_ln_b, w_out, norm2_g, w_gate, w_up, w_down, final_norm_g):
    bsz, seq, d = x.shape
    assert d == D_MODEL and seq % TM == 0
    bf16 = jnp.bfloat16
    row = lambda a: a.reshape(1, -1)
    bs_tile = jnp.repeat(gmlp_b_s.T, A_HEAD_DIM, axis=1)
    args = (
        x, row(norm1_g), w_in.astype(bf16), row(gmlp_ln_g), row(gmlp_ln_b), gmlp_w_s, bs_tile,
        conv_w, row(conv_b), row(conv_ln_g), row(conv_ln_b), w_out.astype(bf16), row(norm2_g),
        w_gate.astype(bf16), w_up.astype(bf16), w_down.astype(bf16), row(final_norm_g),
    )
    x_spec = pl.BlockSpec((1, TM, D_MODEL), lambda b, s: (b, s, 0))
    in_specs = [x_spec] + [_const_spec(a.shape) for a in args[1:]]
    return pl.pallas_call(
        _block_kernel,
        grid=(bsz, seq // TM),
        in_specs=in_specs,
        out_specs=pl.BlockSpec((1, TM, D_MODEL), lambda b, s: (b, s, 0)),
        out_shape=jax.ShapeDtypeStruct(x.shape, x.dtype),
        scratch_shapes=[
            pltpu.VMEM((TM + HALO, B_WIDTH), jnp.float32),
            pltpu.VMEM((TM, D_MODEL), jnp.bfloat16),
        ],
        compiler_params=pltpu.CompilerParams(
            dimension_semantics=("arbitrary", "arbitrary"),
            vmem_limit_bytes=VMEM_LIMIT_BYTES),
        name="hybrid_block",
    )(*args)
```

```python
import functools

import jax
import jax.numpy as jnp
from jax import lax
from jax.experimental import pallas as pl
from jax.experimental.pallas import tpu as pltpu

D_MODEL = 1024
A_WIDTH = 512
B_WIDTH = 512
GMLP_BLOCK = 128
A_HEADS = 4
A_HEAD_DIM = A_WIDTH // A_HEADS
CHUNK = 64
CONV_WIDTH = 31
FFN_HIDDEN = 2816
RMS_EPS = 1e-6
LN_EPS = 1e-5

TM = 512
HALO = 32
CONV_ROWS = 64
LANES = 128
B_LANE_GROUPS = B_WIDTH // LANES
VMEM_LIMIT_BYTES = 58 * 1024 * 1024


def _rms(x, g):
    ms = jnp.mean(x * x, axis=-1, keepdims=True)
    return x * lax.rsqrt(ms + RMS_EPS) * g


def _ln(x, g, b):
    mu = jnp.mean(x, axis=-1, keepdims=True)
    xc = x - mu
    var = jnp.mean(xc * xc, axis=-1, keepdims=True)
    return xc * lax.rsqrt(var + LN_EPS) * g + b


def _sigmoid(x):
    return 0.5 * (jnp.tanh(0.5 * x) + 1.0)


def _block_kernel(x_ref, n1g_ref, w_in_ref, alng_ref, alnb_ref, ws_ref, bs_ref,
                  cw_ref, cb_ref, clng_ref, clnb_ref, w_out_ref, n2g_ref,
                  w_gate_ref, w_up_ref, w_down_ref, fng_ref,
                  o_ref, hbuf_ref, cbuf_ref, y_ref):
    s = pl.program_id(1)

    @pl.when(s == 0)
    def _():
        hbuf_ref[:, 0:HALO, :] = jnp.zeros((B_LANE_GROUPS, HALO, LANES), jnp.float32)

    x = x_ref[0]
    xn = _rms(x, n1g_ref[...]).astype(jnp.bfloat16)
    z = jnp.dot(xn, w_in_ref[...], preferred_element_type=jnp.float32)

    za = jax.nn.gelu(z[:, :2 * A_WIDTH])
    u = za[:, :A_WIDTH]
    v = _ln(za[:, A_WIDTH:], alng_ref[...], alnb_ref[...]).astype(jnp.bfloat16)
    nblk = TM // GMLP_BLOCK
    pos = lax.broadcasted_iota(jnp.int32, (GMLP_BLOCK, GMLP_BLOCK), 0) // CHUNK
    posj = lax.broadcasted_iota(jnp.int32, (GMLP_BLOCK, GMLP_BLOCK), 1) // CHUNK
    mask = posj <= pos
    for h in range(A_HEADS):
        w = jnp.where(mask, ws_ref[h], 0.0).astype(jnp.bfloat16)
        c0 = h * A_HEAD_DIM
        vh = jnp.concatenate(
            [v[n * GMLP_BLOCK:(n + 1) * GMLP_BLOCK, c0:c0 + A_HEAD_DIM] for n in range(nblk)],
            axis=1)
        sg = jnp.dot(w, vh, preferred_element_type=jnp.float32)
        bias = bs_ref[:, c0:c0 + A_HEAD_DIM]
        for n in range(nblk):
            r0 = n * GMLP_BLOCK
            ya = u[r0:r0 + GMLP_BLOCK, c0:c0 + A_HEAD_DIM] * (
                sg[:, n * A_HEAD_DIM:(n + 1) * A_HEAD_DIM] + bias)
            y_ref[r0:r0 + GMLP_BLOCK, c0:c0 + A_HEAD_DIM] = ya.astype(jnp.bfloat16)

    zb = z[:, 2 * A_WIDTH:]
    hh = zb[:, :B_WIDTH] * _sigmoid(zb[:, B_WIDTH:])
    for g in range(B_LANE_GROUPS):
        hbuf_ref[g, HALO:HALO + TM, :] = hh[:, g * LANES:(g + 1) * LANES]

    tap0 = HALO - (CONV_WIDTH - 1)
    for g in range(B_LANE_GROUPS):
        def conv_step(i, carry, g=g):
            r0 = pl.multiple_of(i * CONV_ROWS, CONV_ROWS)
            acc = jnp.zeros((CONV_ROWS, LANES), jnp.float32)
            for k in range(CONV_WIDTH):
                acc = acc + (hbuf_ref[g, pl.ds(r0 + tap0 + k, CONV_ROWS), :]
                             * cw_ref[k:k + 1, g * LANES:(g + 1) * LANES])
            cbuf_ref[pl.ds(r0, CONV_ROWS), g * LANES:(g + 1) * LANES] = acc
            return carry
        lax.fori_loop(0, TM // CONV_ROWS, conv_step, 0)
        hbuf_ref[g, 0:HALO, :] = hbuf_ref[g, TM:TM + HALO, :]

    hn = _ln(cbuf_ref[...] + cb_ref[...], clng_ref[...], clnb_ref[...])
    y_ref[:, A_WIDTH:] = (hn * _sigmoid(hn)).astype(jnp.bfloat16)

    h1 = x + jnp.dot(y_ref[...], w_out_ref[...], preferred_element_type=jnp.float32)

    hn2 = _rms(h1, n2g_ref[...]).astype(jnp.bfloat16)
    gate = jnp.dot(hn2, w_gate_ref[...], preferred_element_type=jnp.float32)
    up = jnp.dot(hn2, w_up_ref[...], preferred_element_type=jnp.float32)
    act = (gate * _sigmoid(gate) * up).astype(jnp.bfloat16)
    h2 = h1 + jnp.dot(act, w_down_ref[...], preferred_element_type=jnp.float32)

    o_ref[0] = _rms(h2, fng_ref[...])


def _const_spec(shape):
    nd = len(shape)
    return pl.BlockSpec(shape, lambda b, s: (0,) * nd, pipeline_mode=pl.Buffered(1))


def kernel(x, norm1_g, w_in, gmlp_ln_g, gmlp_ln_b, gmlp_w_s, gmlp_b_s, conv_w, conv_b,
           conv_ln_g, conv_ln_b, w_out, norm2_g, w_gate, w_up, w_down, final_norm_g):
    bsz, seq, d = x.shape
    assert d == D_MODEL and seq % TM == 0
    bf16 = jnp.bfloat16
    row = lambda a: a.reshape(1, -1)
    bs_tile = jnp.repeat(gmlp_b_s.T, A_HEAD_DIM, axis=1)
    args = (
        x, row(norm1_g), w_in.astype(bf16), row(gmlp_ln_g), row(gmlp_ln_b), gmlp_w_s, bs_tile,
        conv_w, row(conv_b), row(conv_ln_g), row(conv_ln_b), w_out.astype(bf16), row(norm2_g),
        w_gate.astype(bf16), w_up.astype(bf16), w_down.astype(bf16), row(final_norm_g),
    )
    x_spec = pl.BlockSpec((1, TM, D_MODEL), lambda b, s: (b, s, 0))
    in_specs = [x_spec] + [_const_spec(a.shape) for a in args[1:]]
    return pl.pallas_call(
        _block_kernel,
        grid=(bsz, seq // TM),
        in_specs=in_specs,
        out_specs=pl.BlockSpec((1, TM, D_MODEL), lambda b, s: (b, s, 0)),
        out_shape=jax.ShapeDtypeStruct(x.shape, x.dtype),
        scratch_shapes=[
            pltpu.VMEM((B_LANE_GROUPS, TM + HALO, LANES), jnp.float32),
            pltpu.VMEM((TM, B_WIDTH), jnp.float32),
            pltpu.VMEM((TM, D_MODEL), jnp.bfloat16),
        ],
        compiler_params=pltpu.CompilerParams(
            dimension_semantics=("arbitrary", "arbitrary"),
            vmem_limit_bytes=VMEM_LIMIT_BYTES),
        name="hybrid_block",
    )(*args)
```

```python
import functools

import jax
import jax.numpy as jnp
from jax import lax
from jax.experimental import pallas as pl
from jax.experimental.pallas import tpu as pltpu

D_MODEL = 1024
A_WIDTH = 512
B_WIDTH = 512
GMLP_BLOCK = 128
A_HEADS = 4
A_HEAD_DIM = A_WIDTH // A_HEADS
CHUNK = 64
CONV_WIDTH = 31
FFN_HIDDEN = 2816
RMS_EPS = 1e-6
LN_EPS = 1e-5

TM = 512
HALO = 32
PIECE_ROWS = 64
LANES = 128
B_LANE_GROUPS = B_WIDTH // LANES
MXU_TILE = 256
PACKED_ROWS = 16
FFN_CHUNK = 512
VMEM_LIMIT_BYTES = 58 * 1024 * 1024


def _rms(x, g):
    ms = jnp.mean(x * x, axis=-1, keepdims=True)
    return x * lax.rsqrt(ms + RMS_EPS) * g


def _ln(x, g, b):
    mu = jnp.mean(x, axis=-1, keepdims=True)
    xc = x - mu
    var = jnp.mean(xc * xc, axis=-1, keepdims=True)
    return xc * lax.rsqrt(var + LN_EPS) * g + b


def _sigmoid(x):
    return 0.5 * (jnp.tanh(0.5 * x) + 1.0)


def _ffn_chunks():
    return [(f0, min(f0 + FFN_CHUNK, FFN_HIDDEN)) for f0 in range(0, FFN_HIDDEN, FFN_CHUNK)]


def _tie(never, w, pieces):
    by_row = {}
    for (kt, nt), piece in pieces.items():
        by_row.setdefault(kt, {})[nt] = piece
    segs = []
    r = 0
    for kt in sorted(by_row):
        r0 = kt * MXU_TILE
        if r0 > r:
            segs.append(w[r:r0])
        cols = []
        c = 0
        for nt in sorted(by_row[kt]):
            c0 = nt * MXU_TILE
            if c0 > c:
                cols.append(w[r0:r0 + PACKED_ROWS, c:c0])
            cols.append(jnp.where(never, by_row[kt][nt].astype(w.dtype),
                                  w[r0:r0 + PACKED_ROWS, c0:c0 + LANES]))
            c = c0 + LANES
        if c < w.shape[1]:
            cols.append(w[r0:r0 + PACKED_ROWS, c:])
        segs.append(jnp.concatenate(cols, axis=1))
        r = r0 + PACKED_ROWS
    if r < w.shape[0]:
        segs.append(w[r:])
    return jnp.concatenate(segs, axis=0) if len(segs) > 1 else segs[0]


def _block_kernel(tiles_per_seq,
                  x_ref, xp_ref, n1g_ref, w_in_ref, alng_ref, alnb_ref, ws_ref, bs_ref,
                  cw_ref, cb_ref, clng_ref, clnb_ref, w_out_ref, n2g_ref,
                  w_gate_ref, w_up_ref, w_down_ref, fng_ref,
                  o_ref, hbuf_ref, cbuf_ref, y_ref):
    t = pl.program_id(0)
    never = t < 0
    f32 = jnp.float32
    bf16 = jnp.bfloat16

    @pl.when(t == 0)
    def _():
        y_ref[...] = jnp.zeros((TM, D_MODEL), bf16)

    @pl.when(t % tiles_per_seq == 0)
    def _():
        hbuf_ref[:, 0:HALO, :] = jnp.zeros((B_LANE_GROUPS, HALO, LANES), f32)

    h1 = xp_ref[...] + jnp.dot(y_ref[...], w_out_ref[...], preferred_element_type=f32)

    xn = _rms(x_ref[...], n1g_ref[...]).astype(bf16)
    z = jnp.dot(xn, w_in_ref[...], preferred_element_type=f32)

    hn2 = _rms(h1, n2g_ref[...]).astype(bf16)

    row_blocks = list(range(0, TM, PIECE_ROWS))
    tap0 = HALO - (CONV_WIDTH - 1)

    def conv_piece(g, r0):
        if r0 == 0:
            zb = z[:, 2 * A_WIDTH + g * LANES:2 * A_WIDTH + (g + 1) * LANES]
            zg = z[:, 2 * A_WIDTH + B_WIDTH + g * LANES:2 * A_WIDTH + B_WIDTH + (g + 1) * LANES]
            hbuf_ref[g, HALO:HALO + TM, :] = zb * _sigmoid(zg)
        acc = jnp.zeros((PIECE_ROWS, LANES), f32)
        for k in range(CONV_WIDTH):
            acc = acc + (hbuf_ref[g, r0 + tap0 + k:r0 + tap0 + k + PIECE_ROWS, :]
                         * cw_ref[k:k + 1, g * LANES:(g + 1) * LANES])
        cbuf_ref[r0:r0 + PIECE_ROWS, g * LANES:(g + 1) * LANES] = acc
        if r0 == row_blocks[-1]:
            hbuf_ref[g, 0:HALO, :] = hbuf_ref[g, TM:TM + HALO, :]
        return acc[0:PACKED_ROWS, :]

    def conv_norm_piece(r0):
        hn = _ln(cbuf_ref[r0:r0 + PIECE_ROWS, :] + cb_ref[...], clng_ref[...], clnb_ref[...])
        yb = hn * _sigmoid(hn)
        y_ref[r0:r0 + PIECE_ROWS, A_WIDTH:] = yb.astype(bf16)
        return yb[0:PACKED_ROWS, 0:LANES]

    u_blocks, v_blocks = {}, {}

    def gelu_piece(r0):
        za = jax.nn.gelu(z[r0:r0 + PIECE_ROWS, :2 * A_WIDTH])
        u_blocks[r0] = za[:, :A_WIDTH]
        vn = _ln(za[:, A_WIDTH:], alng_ref[...], alnb_ref[...])
        v_blocks[r0] = vn.astype(bf16)
        return vn[0:PACKED_ROWS, 0:LANES]

    queue = [functools.partial(conv_piece, g, r0) for g in range(B_LANE_GROUPS) for r0 in row_blocks]
    queue += [functools.partial(conv_norm_piece, r0) for r0 in row_blocks]
    queue += [functools.partial(gelu_piece, r0) for r0 in row_blocks]
    queue = iter(queue)

    def tied(w, tiles):
        pieces = {}
        for tile in tiles:
            piece = next(queue, None)
            if piece is not None:
                pieces[tile] = piece()
        return _tie(never, w, pieces) if pieces else w

    up_tiles = [(kt, 0) for kt in range(0, D_MODEL // MXU_TILE, 2)]
    down_tiles = [(kt, nt) for kt in range(FFN_HIDDEN // MXU_TILE) for nt in range(0, D_MODEL // MXU_TILE, 2)]
    acts = []
    for f0, f1 in _ffn_chunks():
        gate = jnp.dot(hn2, tied(w_gate_ref[:, f0:f1], up_tiles), preferred_element_type=f32)
        up = jnp.dot(hn2, tied(w_up_ref[:, f0:f1], up_tiles), preferred_element_type=f32)
        acts.append((gate * _sigmoid(gate) * up).astype(bf16))
    act = jnp.concatenate(acts, axis=1)
    h2 = h1 + jnp.dot(act, tied(w_down_ref[...], down_tiles), preferred_element_type=f32)
    for piece in queue:
        piece()

    nblk = TM // GMLP_BLOCK
    u = jnp.concatenate([u_blocks[r0] for r0 in row_blocks], axis=0)
    v = jnp.concatenate([v_blocks[r0] for r0 in row_blocks], axis=0)
    pos_i = lax.broadcasted_iota(jnp.int32, (GMLP_BLOCK, GMLP_BLOCK), 0) // CHUNK
    pos_j = lax.broadcasted_iota(jnp.int32, (GMLP_BLOCK, GMLP_BLOCK), 1) // CHUNK
    mask = pos_j <= pos_i
    for h in range(A_HEADS):
        w = jnp.where(mask, ws_ref[h], 0.0).astype(bf16)
        c0 = h * A_HEAD_DIM
        vh = jnp.concatenate(
            [v[n * GMLP_BLOCK:(n + 1) * GMLP_BLOCK, c0:c0 + A_HEAD_DIM] for n in range(nblk)],
            axis=1)
        sg = jnp.dot(w, vh, preferred_element_type=f32)
        bias = bs_ref[:, c0:c0 + A_HEAD_DIM]
        for n in range(nblk):
            r0 = n * GMLP_BLOCK
            ya = u[r0:r0 + GMLP_BLOCK, c0:c0 + A_HEAD_DIM] * (
                sg[:, n * A_HEAD_DIM:(n + 1) * A_HEAD_DIM] + bias)
            y_ref[r0:r0 + GMLP_BLOCK, c0:c0 + A_HEAD_DIM] = ya.astype(bf16)

    o_ref[...] = _rms(h2, fng_ref[...])


def _const_spec(shape):
    nd = len(shape)
    return pl.BlockSpec(shape, lambda t: (0,) * nd, pipeline_mode=pl.Buffered(1))


def kernel(x, norm1_g, w_in, gmlp_ln_g, gmlp_ln_b, gmlp_w_s, gmlp_b_s, conv_w, conv_b,
           conv_ln_g, conv_ln_b, w_out, norm2_g, w_gate, w_up, w_down, final_norm_g):
    bsz, seq, d = x.shape
    assert d == D_MODEL and seq % TM == 0
    tiles_per_seq = seq // TM
    n_tiles = bsz * tiles_per_seq
    bf16 = jnp.bfloat16
    row = lambda a: a.reshape(1, -1)
    bs_tile = jnp.repeat(gmlp_b_s.T, A_HEAD_DIM, axis=1)
    x2 = x.reshape(bsz * seq, d)
    consts = (
        row(norm1_g), w_in.astype(bf16), row(gmlp_ln_g), row(gmlp_ln_b), gmlp_w_s, bs_tile,
        conv_w, row(conv_b), row(conv_ln_g), row(conv_ln_b), w_out.astype(bf16), row(norm2_g),
        w_gate.astype(bf16), w_up.astype(bf16), w_down.astype(bf16), row(final_norm_g),
    )
    cur_spec = pl.BlockSpec((TM, D_MODEL), lambda t: (jnp.minimum(t, n_tiles - 1), 0))
    prev_spec = pl.BlockSpec((TM, D_MODEL), lambda t: (jnp.maximum(t - 1, 0), 0))
    in_specs = [cur_spec, prev_spec] + [_const_spec(a.shape) for a in consts]
    out = pl.pallas_call(
        functools.partial(_block_kernel, tiles_per_seq),
        grid=(n_tiles + 1,),
        in_specs=in_specs,
        out_specs=prev_spec,
        out_shape=jax.ShapeDtypeStruct(x2.shape, x.dtype),
        scratch_shapes=[
            pltpu.VMEM((B_LANE_GROUPS, TM + HALO, LANES), jnp.float32),
            pltpu.VMEM((TM, B_WIDTH), jnp.float32),
            pltpu.VMEM((TM, D_MODEL), jnp.bfloat16),
        ],
        compiler_params=pltpu.CompilerParams(
            dimension_semantics=("arbitrary",),
            vmem_limit_bytes=VMEM_LIMIT_BYTES),
        name="hybrid_block",
    )(x2, x2, *consts)
    return out.reshape(x.shape)
```

```python
import functools

import jax
import jax.numpy as jnp
from jax import lax
from jax.experimental import pallas as pl
from jax.experimental.pallas import tpu as pltpu

D_MODEL = 1024
A_WIDTH = 512
B_WIDTH = 512
GMLP_BLOCK = 128
A_HEADS = 4
A_HEAD_DIM = A_WIDTH // A_HEADS
CHUNK = 64
CONV_WIDTH = 31
FFN_HIDDEN = 2816
RMS_EPS = 1e-6
LN_EPS = 1e-5

TM = 512
HALO = 32
PIECE_ROWS = 64
LANES = 128
B_LANE_GROUPS = B_WIDTH // LANES
MXU_TILE = 256
PACKED_ROWS = 16
FFN_CHUNK = 512
VMEM_LIMIT_BYTES = 58 * 1024 * 1024


def _rms(x, g):
    ms = jnp.mean(x * x, axis=-1, keepdims=True)
    return x * lax.rsqrt(ms + RMS_EPS) * g


def _ln(x, g, b):
    mu = jnp.mean(x, axis=-1, keepdims=True)
    xc = x - mu
    var = jnp.mean(xc * xc, axis=-1, keepdims=True)
    return xc * lax.rsqrt(var + LN_EPS) * g + b


def _one_plus_tanh(x):
    return jnp.tanh(x) + 1.0


GELU_C = 0.7978845608028654
GELU_K = 0.044715


def _gelu_from_half(xh):
    inner = (2.0 * GELU_C * xh) * (1.0 + (4.0 * GELU_K) * (xh * xh))
    return xh * _one_plus_tanh(inner)


def _ffn_chunks():
    return [(f0, min(f0 + FFN_CHUNK, FFN_HIDDEN)) for f0 in range(0, FFN_HIDDEN, FFN_CHUNK)]


def _sample(x):
    rows, cols = x.shape
    blocks = [x[r:r + PACKED_ROWS, c:c + LANES]
              for r in range(0, rows, PACKED_ROWS) for c in range(0, cols, LANES)]
    while len(blocks) > 1:
        blocks = [blocks[i] + blocks[i + 1] if i + 1 < len(blocks) else blocks[i]
                  for i in range(0, len(blocks), 2)]
    return blocks[0]


def _tie(never, w, pieces):
    by_row = {}
    for (kt, nt), piece in pieces.items():
        by_row.setdefault(kt, {})[nt] = piece
    segs = []
    r = 0
    for kt in sorted(by_row):
        r0 = kt * MXU_TILE
        if r0 > r:
            segs.append(w[r:r0])
        cols = []
        c = 0
        for nt in sorted(by_row[kt]):
            c0 = nt * MXU_TILE
            if c0 > c:
                cols.append(w[r0:r0 + PACKED_ROWS, c:c0])
            cols.append(jnp.where(never, by_row[kt][nt].astype(w.dtype),
                                  w[r0:r0 + PACKED_ROWS, c0:c0 + LANES]))
            c = c0 + LANES
        if c < w.shape[1]:
            cols.append(w[r0:r0 + PACKED_ROWS, c:])
        segs.append(jnp.concatenate(cols, axis=1))
        r = r0 + PACKED_ROWS
    if r < w.shape[0]:
        segs.append(w[r:])
    return jnp.concatenate(segs, axis=0) if len(segs) > 1 else segs[0]


def _block_kernel(tiles_per_seq,
                  x_ref, n1g_ref, w_in_ref, alng_ref, alnb_ref, ws_ref, bs_ref,
                  cw_ref, cb_ref, clng_ref, clnb_ref, w_out_ref, n2g_ref,
                  w_gate_ref, w_up_ref, w_down_ref, fng_ref,
                  o_ref, hbuf_ref, cbuf_ref, y_ref, xs_ref):
    t = pl.program_id(0)
    never = t < 0
    f32 = jnp.float32
    bf16 = jnp.bfloat16

    @pl.when(t == 0)
    def _():
        y_ref[...] = jnp.zeros((TM, D_MODEL), bf16)
        xs_ref[...] = jnp.zeros((TM, D_MODEL), f32)

    @pl.when(t % tiles_per_seq == 0)
    def _():
        hbuf_ref[:, 0:HALO, :] = jnp.zeros((B_LANE_GROUPS, HALO, LANES), f32)

    h1 = xs_ref[...] + jnp.dot(y_ref[...], w_out_ref[...], preferred_element_type=f32)
    o_ref[...] = h1

    x = x_ref[...]
    xs_ref[...] = x
    xn = _rms(x, n1g_ref[...]).astype(bf16)
    z = jnp.dot(xn, w_in_ref[...], preferred_element_type=f32)

    hn2 = _rms(h1, n2g_ref[...]).astype(bf16)

    row_blocks = list(range(0, TM, PIECE_ROWS))
    tap0 = HALO - (CONV_WIDTH - 1)

    def conv_piece(g, r0, after=None):
        if r0 == 0:
            zb = z[:, 2 * A_WIDTH + g * LANES:2 * A_WIDTH + (g + 1) * LANES]
            zg = z[:, 2 * A_WIDTH + B_WIDTH + g * LANES:2 * A_WIDTH + B_WIDTH + (g + 1) * LANES]
            hbuf_ref[g, HALO:HALO + TM, :] = zb * _one_plus_tanh(zg)
        acc = jnp.zeros((PIECE_ROWS, LANES), f32)
        if after is not None:
            acc = jnp.where(never, after, acc)
        for k in range(CONV_WIDTH):
            acc = acc + (hbuf_ref[g, r0 + tap0 + k:r0 + tap0 + k + PIECE_ROWS, :]
                         * cw_ref[k:k + 1, g * LANES:(g + 1) * LANES])
        cbuf_ref[r0:r0 + PIECE_ROWS, g * LANES:(g + 1) * LANES] = acc
        if r0 == row_blocks[-1]:
            hbuf_ref[g, 0:HALO, :] = hbuf_ref[g, TM:TM + HALO, :]
        return _sample(acc)

    def conv_norm_piece(r0, after=None):
        cv = cbuf_ref[r0:r0 + PIECE_ROWS, :]
        if after is not None:
            cv = jnp.concatenate([jnp.where(never, after, cv[:, 0:LANES]), cv[:, LANES:]], axis=1)
        hn = _ln(cv + cb_ref[...], clng_ref[...], clnb_ref[...])
        yb = hn * _one_plus_tanh(hn)
        y_ref[r0:r0 + PIECE_ROWS, A_WIDTH:] = yb.astype(bf16)
        return _sample(yb)

    u_blocks, v_blocks = {}, {}

    def gelu_piece(r0, after=None):
        zz = z[r0:r0 + PIECE_ROWS, :2 * A_WIDTH]
        if after is not None:
            zz = jnp.concatenate([jnp.where(never, after, zz[:, 0:LANES]), zz[:, LANES:]], axis=1)
        za = _gelu_from_half(zz)
        u_blocks[r0] = za[:, :A_WIDTH]
        vn = _ln(za[:, A_WIDTH:], alng_ref[...], alnb_ref[...])
        v_blocks[r0] = vn.astype(bf16)
        return _sample(za)

    queue = [functools.partial(conv_piece, g, r0) for g in range(B_LANE_GROUPS) for r0 in row_blocks]
    queue += [functools.partial(conv_norm_piece, r0) for r0 in row_blocks]
    queue += [functools.partial(gelu_piece, r0) for r0 in row_blocks]
    queue = iter(queue)

    def tied(w, tiles, result=None):
        pieces = {}
        for i, tile in enumerate(tiles):
            piece = next(queue, None)
            if piece is not None:
                after = None
                if result is not None:
                    r0 = PIECE_ROWS * (i % len(row_blocks))
                    after = result[r0:r0 + PIECE_ROWS, 0:LANES]
                pieces[tile] = piece(after=after)
        return _tie(never, w, pieces) if pieces else w

    k_tiles = D_MODEL // MXU_TILE
    up_tiles = [(kt, 0) for kt in range(k_tiles)] + [(k_tiles - 1, 1)]
    down_tiles = [(kt, nt) for kt in range(1, FFN_HIDDEN // MXU_TILE) for nt in range(0, D_MODEL // MXU_TILE, 2)]
    acts = []
    result = None
    for f0, f1 in _ffn_chunks():
        gate = jnp.dot(hn2, w_gate_ref[:, f0:f1], preferred_element_type=f32)
        tiles = [tl for tl in up_tiles if (tl[1] + 1) * MXU_TILE <= f1 - f0]
        up = jnp.dot(hn2, tied(w_up_ref[:, f0:f1], tiles, result), preferred_element_type=f32)
        acts.append((_one_plus_tanh(gate) * (gate * up)).astype(bf16))
        result = gate
    act = jnp.concatenate(acts, axis=1)
    ffn = jnp.dot(act, tied(w_down_ref[...], down_tiles, result), preferred_element_type=f32)
    for piece in queue:
        piece()

    nblk = TM // GMLP_BLOCK
    u = jnp.concatenate([u_blocks[r0] for r0 in row_blocks], axis=0)
    v = jnp.concatenate([v_blocks[r0] for r0 in row_blocks], axis=0)
    pos_i = lax.broadcasted_iota(jnp.int32, (GMLP_BLOCK, GMLP_BLOCK), 0) // CHUNK
    pos_j = lax.broadcasted_iota(jnp.int32, (GMLP_BLOCK, GMLP_BLOCK), 1) // CHUNK
    mask = pos_j <= pos_i
    for h in range(A_HEADS):
        w = jnp.where(mask, ws_ref[h], 0.0).astype(bf16)
        c0 = h * A_HEAD_DIM
        vh = jnp.concatenate(
            [v[n * GMLP_BLOCK:(n + 1) * GMLP_BLOCK, c0:c0 + A_HEAD_DIM] for n in range(nblk)],
            axis=1)
        sg = jnp.dot(w, vh, preferred_element_type=f32)
        bias = bs_ref[:, c0:c0 + A_HEAD_DIM]
        for n in range(nblk):
            r0 = n * GMLP_BLOCK
            ya = u[r0:r0 + GMLP_BLOCK, c0:c0 + A_HEAD_DIM] * (
                sg[:, n * A_HEAD_DIM:(n + 1) * A_HEAD_DIM] + bias)
            y_ref[r0:r0 + GMLP_BLOCK, c0:c0 + A_HEAD_DIM] = ya.astype(bf16)

    o_ref[...] = _rms(o_ref[...] + ffn, fng_ref[...])


def _const_spec(shape):
    nd = len(shape)
    return pl.BlockSpec(shape, lambda t: (0,) * nd, pipeline_mode=pl.Buffered(1))


def kernel(x, norm1_g, w_in, gmlp_ln_g, gmlp_ln_b, gmlp_w_s, gmlp_b_s, conv_w, conv_b,
           conv_ln_g, conv_ln_b, w_out, norm2_g, w_gate, w_up, w_down, final_norm_g):
    bsz, seq, d = x.shape
    assert d == D_MODEL and seq % TM == 0
    tiles_per_seq = seq // TM
    n_tiles = bsz * tiles_per_seq
    bf16 = jnp.bfloat16
    row = lambda a: a.reshape(1, -1)
    bs_tile = jnp.repeat(gmlp_b_s.T, A_HEAD_DIM, axis=1)
    x2 = x.reshape(bsz * seq, d)
    half = lambda a: (0.5 * a)
    consts = (
        row(norm1_g), half(w_in).astype(bf16), row(gmlp_ln_g), row(gmlp_ln_b), gmlp_w_s, bs_tile,
        conv_w, row(conv_b), row(half(conv_ln_g)), row(half(conv_ln_b)), w_out.astype(bf16), row(norm2_g),
        half(w_gate).astype(bf16), w_up.astype(bf16), w_down.astype(bf16), row(final_norm_g),
    )
    cur_spec = pl.BlockSpec((TM, D_MODEL), lambda t: (jnp.minimum(t, n_tiles - 1), 0))
    prev_spec = pl.BlockSpec((TM, D_MODEL), lambda t: (jnp.maximum(t - 1, 0), 0))
    in_specs = [cur_spec] + [_const_spec(a.shape) for a in consts]
    out = pl.pallas_call(
        functools.partial(_block_kernel, tiles_per_seq),
        grid=(n_tiles + 1,),
        in_specs=in_specs,
        out_specs=prev_spec,
        out_shape=jax.ShapeDtypeStruct(x2.shape, x.dtype),
        scratch_shapes=[
            pltpu.VMEM((B_LANE_GROUPS, TM + HALO, LANES), jnp.float32),
            pltpu.VMEM((TM, B_WIDTH), jnp.float32),
            pltpu.VMEM((TM, D_MODEL), jnp.bfloat16),
            pltpu.VMEM((TM, D_MODEL), jnp.float32),
        ],
        compiler_params=pltpu.CompilerParams(
            dimension_semantics=("arbitrary",),
            vmem_limit_bytes=VMEM_LIMIT_BYTES),
        name="hybrid_block",
    )(x2, *consts)
    return out.reshape(x.shape)
```

```python
import functools

import jax
import jax.numpy as jnp
from jax import lax
from jax.experimental import pallas as pl
from jax.experimental.pallas import tpu as pltpu

D_MODEL = 1024
A_WIDTH = 512
B_WIDTH = 512
GMLP_BLOCK = 128
A_HEADS = 4
A_HEAD_DIM = A_WIDTH // A_HEADS
CHUNK = 64
CONV_WIDTH = 31
FFN_HIDDEN = 2816
RMS_EPS = 1e-6
LN_EPS = 1e-5

TM = 512
HALO = 32
PIECE_ROWS = 32
LANES = 128
B_LANE_GROUPS = B_WIDTH // LANES
MXU_TILE = 256
PACKED_ROWS = 16
FFN_CHUNK = 512
VMEM_LIMIT_BYTES = 58 * 1024 * 1024


def _rms(x, g):
    ms = jnp.mean(x * x, axis=-1, keepdims=True)
    return x * lax.rsqrt(ms + RMS_EPS) * g


def _ln(x, g, b):
    mu = jnp.mean(x, axis=-1, keepdims=True)
    xc = x - mu
    var = jnp.mean(xc * xc, axis=-1, keepdims=True)
    return xc * lax.rsqrt(var + LN_EPS) * g + b


def _one_plus_tanh(x):
    return jnp.tanh(x) + 1.0


GELU_C = 0.7978845608028654
GELU_K = 0.044715


def _gelu_from_half(xh):
    inner = (2.0 * GELU_C * xh) * (1.0 + (4.0 * GELU_K) * (xh * xh))
    return xh * _one_plus_tanh(inner)


def _ffn_chunks():
    return [(f0, min(f0 + FFN_CHUNK, FFN_HIDDEN)) for f0 in range(0, FFN_HIDDEN, FFN_CHUNK)]


def _sample(x):
    rows, cols = x.shape
    blocks = [x[r:r + PACKED_ROWS, c:c + LANES]
              for r in range(0, rows, PACKED_ROWS) for c in range(0, cols, LANES)]
    while len(blocks) > 1:
        blocks = [blocks[i] + blocks[i + 1] if i + 1 < len(blocks) else blocks[i]
                  for i in range(0, len(blocks), 2)]
    return blocks[0]


def _tie(never, w, pieces):
    by_row = {}
    for (kt, nt), piece in pieces.items():
        by_row.setdefault(kt, {})[nt] = piece
    segs = []
    r = 0
    for kt in sorted(by_row):
        r0 = kt * MXU_TILE
        if r0 > r:
            segs.append(w[r:r0])
        cols = []
        c = 0
        for nt in sorted(by_row[kt]):
            c0 = nt * MXU_TILE
            if c0 > c:
                cols.append(w[r0:r0 + PACKED_ROWS, c:c0])
            cols.append(jnp.where(never, by_row[kt][nt].astype(w.dtype),
                                  w[r0:r0 + PACKED_ROWS, c0:c0 + LANES]))
            c = c0 + LANES
        if c < w.shape[1]:
            cols.append(w[r0:r0 + PACKED_ROWS, c:])
        segs.append(jnp.concatenate(cols, axis=1))
        r = r0 + PACKED_ROWS
    if r < w.shape[0]:
        segs.append(w[r:])
    return jnp.concatenate(segs, axis=0) if len(segs) > 1 else segs[0]


def _block_kernel(tiles_per_seq,
                  x_ref, n1g_ref, w_in_ref, alng_ref, alnb_ref, ws_ref, bs_ref,
                  cw_ref, cb_ref, clng_ref, clnb_ref, w_out_ref, n2g_ref,
                  w_gate_ref, w_up_ref, w_down_ref, fng_ref,
                  o_ref, hbuf_ref, cbuf_ref, y_ref, xs_ref):
    t = pl.program_id(0)
    never = t < 0
    f32 = jnp.float32
    bf16 = jnp.bfloat16

    @pl.when(t == 0)
    def _():
        y_ref[...] = jnp.zeros((TM, D_MODEL), bf16)
        xs_ref[...] = jnp.zeros((TM, D_MODEL), f32)

    @pl.when(t % tiles_per_seq == 0)
    def _():
        hbuf_ref[:, 0:HALO, :] = jnp.zeros((B_LANE_GROUPS, HALO, LANES), f32)

    h1 = xs_ref[...] + jnp.dot(y_ref[...], w_out_ref[...], preferred_element_type=f32)
    o_ref[...] = h1

    x = x_ref[...]
    xs_ref[...] = x
    xn = _rms(x, n1g_ref[...]).astype(bf16)
    z = jnp.dot(xn, w_in_ref[...], preferred_element_type=f32)

    hn2 = _rms(h1, n2g_ref[...]).astype(bf16)

    row_blocks = list(range(0, TM, PIECE_ROWS))
    tap0 = HALO - (CONV_WIDTH - 1)

    def conv_piece(g, r0, after=None):
        if r0 == 0:
            zb = z[:, 2 * A_WIDTH + g * LANES:2 * A_WIDTH + (g + 1) * LANES]
            zg = z[:, 2 * A_WIDTH + B_WIDTH + g * LANES:2 * A_WIDTH + B_WIDTH + (g + 1) * LANES]
            hbuf_ref[g, HALO:HALO + TM, :] = zb * _one_plus_tanh(zg)
        acc = jnp.zeros((PIECE_ROWS, LANES), f32)
        if after is not None:
            acc = jnp.where(never, after, acc)
        for k in range(CONV_WIDTH):
            acc = acc + (hbuf_ref[g, r0 + tap0 + k:r0 + tap0 + k + PIECE_ROWS, :]
                         * cw_ref[k:k + 1, g * LANES:(g + 1) * LANES])
        cbuf_ref[r0:r0 + PIECE_ROWS, g * LANES:(g + 1) * LANES] = acc
        if r0 == row_blocks[-1]:
            hbuf_ref[g, 0:HALO, :] = hbuf_ref[g, TM:TM + HALO, :]
        return _sample(acc)

    half_clng = 0.5 * clng_ref[...]
    half_clnb = 0.5 * clnb_ref[...]

    def conv_norm_piece(r0, after=None):
        cv = cbuf_ref[r0:r0 + PIECE_ROWS, :]
        if after is not None:
            cv = jnp.concatenate([jnp.where(never, after, cv[:, 0:LANES]), cv[:, LANES:]], axis=1)
        hn = _ln(cv + cb_ref[...], half_clng, half_clnb)
        yb = hn * _one_plus_tanh(hn)
        y_ref[r0:r0 + PIECE_ROWS, A_WIDTH:] = yb.astype(bf16)
        return _sample(yb)

    u_blocks, v_blocks = {}, {}

    def gelu_piece(r0, after=None):
        zz = z[r0:r0 + PIECE_ROWS, :2 * A_WIDTH]
        if after is not None:
            zz = jnp.concatenate([jnp.where(never, after, zz[:, 0:LANES]), zz[:, LANES:]], axis=1)
        za = _gelu_from_half(zz)
        u_blocks[r0] = za[:, :A_WIDTH]
        vn = _ln(za[:, A_WIDTH:], alng_ref[...], alnb_ref[...])
        v_blocks[r0] = vn.astype(bf16)
        return _sample(za)

    queue = [functools.partial(conv_piece, g, r0) for g in range(B_LANE_GROUPS) for r0 in row_blocks]
    queue += [functools.partial(conv_norm_piece, r0) for r0 in row_blocks]
    queue += [functools.partial(gelu_piece, r0) for r0 in row_blocks]
    queue = iter(queue)

    def tied(w, tiles, result=None):
        pieces = {}
        for i, tile in enumerate(tiles):
            piece = next(queue, None)
            if piece is not None:
                after = None
                if result is not None:
                    r0 = PIECE_ROWS * (i % len(row_blocks))
                    after = result[r0:r0 + PIECE_ROWS, 0:LANES]
                pieces[tile] = piece(after=after)
        return _tie(never, w, pieces) if pieces else w

    k_tiles = D_MODEL // MXU_TILE
    chunk_tiles = [(kt, 0) for kt in range(k_tiles)] + [(kt, 1) for kt in range(1, k_tiles, 2)]
    down_tiles = [(kt, nt) for kt in range(1, FFN_HIDDEN // MXU_TILE) for nt in range(0, D_MODEL // MXU_TILE, 2)]
    acts = []
    result = None
    for f0, f1 in _ffn_chunks():
        tiles = [tl for tl in chunk_tiles if (tl[1] + 1) * MXU_TILE <= f1 - f0]
        gate = jnp.dot(hn2, tied(w_gate_ref[:, f0:f1], tiles, result), preferred_element_type=f32)
        up = jnp.dot(hn2, tied(w_up_ref[:, f0:f1], tiles, result), preferred_element_type=f32)
        acts.append((_one_plus_tanh(gate) * (gate * up)).astype(bf16))
        result = gate
    act = jnp.concatenate(acts, axis=1)
    ffn = jnp.dot(act, tied(w_down_ref[...], down_tiles, result), preferred_element_type=f32)
    for piece in queue:
        piece()

    nblk = TM // GMLP_BLOCK
    u = jnp.concatenate([u_blocks[r0] for r0 in row_blocks], axis=0)
    v = jnp.concatenate([v_blocks[r0] for r0 in row_blocks], axis=0)
    pos_i = lax.broadcasted_iota(jnp.int32, (GMLP_BLOCK, GMLP_BLOCK), 0) // CHUNK
    pos_j = lax.broadcasted_iota(jnp.int32, (GMLP_BLOCK, GMLP_BLOCK), 1) // CHUNK
    mask = pos_j <= pos_i
    for h in range(A_HEADS):
        w = jnp.where(mask, ws_ref[h], 0.0).astype(bf16)
        c0 = h * A_HEAD_DIM
        vh = jnp.concatenate(
            [v[n * GMLP_BLOCK:(n + 1) * GMLP_BLOCK, c0:c0 + A_HEAD_DIM] for n in range(nblk)],
            axis=1)
        sg = jnp.dot(w, vh, preferred_element_type=f32)
        bias = bs_ref[:, c0:c0 + A_HEAD_DIM]
        for n in range(nblk):
            r0 = n * GMLP_BLOCK
            ya = u[r0:r0 + GMLP_BLOCK, c0:c0 + A_HEAD_DIM] * (
                sg[:, n * A_HEAD_DIM:(n + 1) * A_HEAD_DIM] + bias)
            y_ref[r0:r0 + GMLP_BLOCK, c0:c0 + A_HEAD_DIM] = ya.astype(bf16)

    o_ref[...] = _rms(o_ref[...] + ffn, fng_ref[...])


def _const_spec(shape):
    nd = len(shape)
    return pl.BlockSpec(shape, lambda t: (0,) * nd, pipeline_mode=pl.Buffered(1))


CAST_STEPS = 8


def _cast_kernel(scales, *refs):
    n = len(scales)
    for scale, src, dst in zip(scales, refs[:n], refs[n:]):
        v = src[...]
        dst[...] = (v if scale == 1.0 else scale * v).astype(dst.dtype)


def _cast_weights(weights, scales):
    specs = []
    for w in weights:
        rows, cols = w.shape
        assert rows % (CAST_STEPS * PACKED_ROWS) == 0
        specs.append(pl.BlockSpec((rows // CAST_STEPS, cols), lambda i: (i, 0)))
    return pl.pallas_call(
        functools.partial(_cast_kernel, tuple(scales)),
        grid=(CAST_STEPS,),
        in_specs=specs,
        out_specs=specs,
        out_shape=[jax.ShapeDtypeStruct(w.shape, jnp.bfloat16) for w in weights],
        compiler_params=pltpu.CompilerParams(dimension_semantics=("arbitrary",)),
        name="cast_weights",
    )(*weights)


def kernel(x, norm1_g, w_in, gmlp_ln_g, gmlp_ln_b, gmlp_w_s, gmlp_b_s, conv_w, conv_b,
           conv_ln_g, conv_ln_b, w_out, norm2_g, w_gate, w_up, w_down, final_norm_g):
    bsz, seq, d = x.shape
    assert d == D_MODEL and seq % TM == 0
    tiles_per_seq = seq // TM
    n_tiles = bsz * tiles_per_seq
    bf16 = jnp.bfloat16
    row = lambda a: a.reshape(1, -1)
    bs_tile = jnp.repeat(gmlp_b_s.T, A_HEAD_DIM, axis=1)
    x2 = x.reshape(bsz * seq, d)
    w_in_h, w_out_b, w_gate_h, w_up_b, w_down_b = _cast_weights(
        (w_in, w_out, w_gate, w_up, w_down), (0.5, 1.0, 0.5, 1.0, 1.0))
    consts = (
        row(norm1_g), w_in_h, row(gmlp_ln_g), row(gmlp_ln_b), gmlp_w_s, bs_tile,
        conv_w, row(conv_b), row(conv_ln_g), row(conv_ln_b), w_out_b, row(norm2_g),
        w_gate_h, w_up_b, w_down_b, row(final_norm_g),
    )
    cur_spec = pl.BlockSpec((TM, D_MODEL), lambda t: (jnp.minimum(t, n_tiles - 1), 0))
    prev_spec = pl.BlockSpec((TM, D_MODEL), lambda t: (jnp.maximum(t - 1, 0), 0))
    in_specs = [cur_spec] + [_const_spec(a.shape) for a in consts]
    out = pl.pallas_call(
        functools.partial(_block_kernel, tiles_per_seq),
        grid=(n_tiles + 1,),
        in_specs=in_specs,
        out_specs=prev_spec,
        out_shape=jax.ShapeDtypeStruct(x2.shape, x.dtype),
        scratch_shapes=[
            pltpu.VMEM((B_LANE_GROUPS, TM + HALO, LANES), jnp.float32),
            pltpu.VMEM((TM, B_WIDTH), jnp.float32),
            pltpu.VMEM((TM, D_MODEL), jnp.bfloat16),
            pltpu.VMEM((TM, D_MODEL), jnp.float32),
        ],
        compiler_params=pltpu.CompilerParams(
            dimension_semantics=("arbitrary",),
            vmem_limit_bytes=VMEM_LIMIT_BYTES),
        name="hybrid_block",
    )(x2, *consts)
    return out.reshape(x.shape)
```

```python
import functools

import jax
import jax.numpy as jnp
from jax import lax
from jax.experimental import pallas as pl
from jax.experimental.pallas import tpu as pltpu

D_MODEL = 1024
A_WIDTH = 512
B_WIDTH = 512
IN_WIDTH = 2 * A_WIDTH + 2 * B_WIDTH
GMLP_BLOCK = 128
A_HEADS = 4
A_HEAD_DIM = A_WIDTH // A_HEADS
CHUNK = 64
CONV_WIDTH = 31
FFN_HIDDEN = 2816
RMS_EPS = 1e-6
LN_EPS = 1e-5

TM = 512
HALO = 32
PIECE_ROWS = 32
LANES = 128
B_LANE_GROUPS = B_WIDTH // LANES
MXU_TILE = 256
PACKED_ROWS = 16
FFN_CHUNK = 512
STAGE_ROWS = 512
STAGE_COLS = 1024
STAGE_SLOTS = 3
VMEM_LIMIT_BYTES = 58 * 1024 * 1024

WEIGHTS = (
    ("w_in", D_MODEL, IN_WIDTH, 0.5),
    ("w_out", D_MODEL, D_MODEL, 1.0),
    ("w_gate", D_MODEL, FFN_HIDDEN, 0.5),
    ("w_up", D_MODEL, FFN_HIDDEN, 1.0),
    ("w_down", FFN_HIDDEN, D_MODEL, 1.0),
)


def _rms(x, g):
    ms = jnp.mean(x * x, axis=-1, keepdims=True)
    return x * lax.rsqrt(ms + RMS_EPS) * g


def _ln(x, g, b):
    mu = jnp.mean(x, axis=-1, keepdims=True)
    xc = x - mu
    var = jnp.mean(xc * xc, axis=-1, keepdims=True)
    return xc * lax.rsqrt(var + LN_EPS) * g + b


def _one_plus_tanh(x):
    return jnp.tanh(x) + 1.0


GELU_C = 0.7978845608028654
GELU_K = 0.044715


def _gelu_from_half(xh):
    inner = (2.0 * GELU_C * xh) * (1.0 + (4.0 * GELU_K) * (xh * xh))
    return xh * _one_plus_tanh(inner)


def _ffn_chunks():
    return [(f0, min(f0 + FFN_CHUNK, FFN_HIDDEN)) for f0 in range(0, FFN_HIDDEN, FFN_CHUNK)]


def _sample(x):
    rows, cols = x.shape
    blocks = [x[r:r + PACKED_ROWS, c:c + LANES]
              for r in range(0, rows, PACKED_ROWS) for c in range(0, cols, LANES)]
    while len(blocks) > 1:
        blocks = [blocks[i] + blocks[i + 1] if i + 1 < len(blocks) else blocks[i]
                  for i in range(0, len(blocks), 2)]
    return blocks[0]


def _tie(never, w, pieces):
    by_row = {}
    for (kt, nt), piece in pieces.items():
        by_row.setdefault(kt, {})[nt] = piece
    segs = []
    r = 0
    for kt in sorted(by_row):
        r0 = kt * MXU_TILE
        if r0 > r:
            segs.append(w[r:r0])
        cols = []
        c = 0
        for nt in sorted(by_row[kt]):
            c0 = nt * MXU_TILE
            if c0 > c:
                cols.append(w[r0:r0 + PACKED_ROWS, c:c0])
            cols.append(jnp.where(never, by_row[kt][nt].astype(w.dtype),
                                  w[r0:r0 + PACKED_ROWS, c0:c0 + LANES]))
            c = c0 + LANES
        if c < w.shape[1]:
            cols.append(w[r0:r0 + PACKED_ROWS, c:])
        segs.append(jnp.concatenate(cols, axis=1))
        r = r0 + PACKED_ROWS
    if r < w.shape[0]:
        segs.append(w[r:])
    return jnp.concatenate(segs, axis=0) if len(segs) > 1 else segs[0]


def _weight_chunks():
    chunks = []
    for wi, (_, rows, cols, _) in enumerate(WEIGHTS):
        for r0 in range(0, rows, STAGE_ROWS):
            for c0 in range(0, cols, STAGE_COLS):
                chunks.append((wi, r0, min(STAGE_ROWS, rows - r0), c0, min(STAGE_COLS, cols - c0)))
    return chunks


def _load_weights(w_hbm, w_vmem, stage_ref, sems):
    chunks = _weight_chunks()

    def copy(i):
        wi, r0, nr, c0, nc = chunks[i]
        slot = i % STAGE_SLOTS
        return pltpu.make_async_copy(
            w_hbm[wi].at[pl.ds(r0, nr), pl.ds(c0, nc)],
            stage_ref.at[slot, pl.ds(0, nr), pl.ds(0, nc)],
            sems.at[slot])

    for i in range(min(STAGE_SLOTS - 1, len(chunks))):
        copy(i).start()
    for i, (wi, r0, nr, c0, nc) in enumerate(chunks):
        if i + STAGE_SLOTS - 1 < len(chunks):
            copy(i + STAGE_SLOTS - 1).start()
        copy(i).wait()
        v = stage_ref[i % STAGE_SLOTS, 0:nr, 0:nc]
        scale = WEIGHTS[wi][3]
        w_vmem[wi][r0:r0 + nr, c0:c0 + nc] = (v if scale == 1.0 else scale * v).astype(jnp.bfloat16)


def _step(do_mixer, do_channel, t,
          x_ref, n1g_ref, alng_ref, alnb_ref, ws_ref, bs_ref, cw_ref, cb_ref, clng_ref, clnb_ref,
          n2g_ref, fng_ref, w_in_ref, w_out_ref, w_gate_ref, w_up_ref, w_down_ref,
          o_ref, hbuf_ref, cbuf_ref, y_ref, xs_ref):
    never = t < 0
    f32 = jnp.float32
    bf16 = jnp.bfloat16
    tie_pieces = do_mixer and do_channel

    if do_channel:
        h1 = xs_ref[...] + jnp.dot(y_ref[...], w_out_ref[...], preferred_element_type=f32)
        o_ref[...] = h1

    queue = iter(())
    if do_mixer:
        x = x_ref[...]
        xs_ref[...] = x
        xn = _rms(x, n1g_ref[...]).astype(bf16)
        z = jnp.dot(xn, w_in_ref[...], preferred_element_type=f32)

    if do_channel:
        hn2 = _rms(h1, n2g_ref[...]).astype(bf16)

    row_blocks = list(range(0, TM, PIECE_ROWS))
    if do_mixer:
        tap0 = HALO - (CONV_WIDTH - 1)

        def conv_piece(g, r0, after=None):
            if r0 == 0:
                zb = z[:, 2 * A_WIDTH + g * LANES:2 * A_WIDTH + (g + 1) * LANES]
                zg = z[:, 2 * A_WIDTH + B_WIDTH + g * LANES:2 * A_WIDTH + B_WIDTH + (g + 1) * LANES]
                hbuf_ref[g, HALO:HALO + TM, :] = zb * _one_plus_tanh(zg)
            acc = jnp.zeros((PIECE_ROWS, LANES), f32)
            if after is not None:
                acc = jnp.where(never, after, acc)
            for k in range(CONV_WIDTH):
                acc = acc + (hbuf_ref[g, r0 + tap0 + k:r0 + tap0 + k + PIECE_ROWS, :]
                             * cw_ref[k:k + 1, g * LANES:(g + 1) * LANES])
            cbuf_ref[r0:r0 + PIECE_ROWS, g * LANES:(g + 1) * LANES] = acc
            if r0 == row_blocks[-1]:
                hbuf_ref[g, 0:HALO, :] = hbuf_ref[g, TM:TM + HALO, :]
            return _sample(acc) if tie_pieces else None

        half_clng = 0.5 * clng_ref[...]
        half_clnb = 0.5 * clnb_ref[...]

        def conv_norm_piece(r0, after=None):
            cv = cbuf_ref[r0:r0 + PIECE_ROWS, :]
            if after is not None:
                cv = jnp.concatenate([jnp.where(never, after, cv[:, 0:LANES]), cv[:, LANES:]], axis=1)
            hn = _ln(cv + cb_ref[...], half_clng, half_clnb)
            yb = hn * _one_plus_tanh(hn)
            y_ref[r0:r0 + PIECE_ROWS, A_WIDTH:] = yb.astype(bf16)
            return _sample(yb) if tie_pieces else None

        u_blocks, v_blocks = {}, {}

        def gelu_piece(r0, after=None):
            zz = z[r0:r0 + PIECE_ROWS, :2 * A_WIDTH]
            if after is not None:
                zz = jnp.concatenate([jnp.where(never, after, zz[:, 0:LANES]), zz[:, LANES:]], axis=1)
            za = _gelu_from_half(zz)
            u_blocks[r0] = za[:, :A_WIDTH]
            vn = _ln(za[:, A_WIDTH:], alng_ref[...], alnb_ref[...])
            v_blocks[r0] = vn.astype(bf16)
            return _sample(za) if tie_pieces else None

        queue = [functools.partial(conv_piece, g, r0) for g in range(B_LANE_GROUPS) for r0 in row_blocks]
        queue += [functools.partial(conv_norm_piece, r0) for r0 in row_blocks]
        queue += [functools.partial(gelu_piece, r0) for r0 in row_blocks]
        queue = iter(queue)

    def tied(w, tiles, result=None):
        pieces = {}
        for i, tile in enumerate(tiles):
            piece = next(queue, None)
            if piece is not None:
                after = None
                if result is not None:
                    r0 = PIECE_ROWS * (i % len(row_blocks))
                    after = result[r0:r0 + PIECE_ROWS, 0:LANES]
                pieces[tile] = piece(after=after)
        return _tie(never, w, pieces) if pieces else w

    if do_channel:
        k_tiles = D_MODEL // MXU_TILE
        chunk_tiles = [(kt, 0) for kt in range(k_tiles)] + [(kt, 1) for kt in range(1, k_tiles, 2)]
        down_tiles = [(kt, nt) for kt in range(1, FFN_HIDDEN // MXU_TILE) for nt in range(0, D_MODEL // MXU_TILE, 2)]
        acts = []
        result = None
        for f0, f1 in _ffn_chunks():
            tiles = [tl for tl in chunk_tiles if (tl[1] + 1) * MXU_TILE <= f1 - f0]
            gate = jnp.dot(hn2, tied(w_gate_ref[:, f0:f1], tiles, result), preferred_element_type=f32)
            up = jnp.dot(hn2, tied(w_up_ref[:, f0:f1], tiles, result), preferred_element_type=f32)
            acts.append((_one_plus_tanh(gate) * (gate * up)).astype(bf16))
            result = gate
        act = jnp.concatenate(acts, axis=1)
        ffn = jnp.dot(act, tied(w_down_ref[...], down_tiles, result), preferred_element_type=f32)
    for piece in queue:
        piece()

    if do_mixer:
        nblk = TM // GMLP_BLOCK
        u = jnp.concatenate([u_blocks[r0] for r0 in row_blocks], axis=0)
        v = jnp.concatenate([v_blocks[r0] for r0 in row_blocks], axis=0)
        pos_i = lax.broadcasted_iota(jnp.int32, (GMLP_BLOCK, GMLP_BLOCK), 0) // CHUNK
        pos_j = lax.broadcasted_iota(jnp.int32, (GMLP_BLOCK, GMLP_BLOCK), 1) // CHUNK
        mask = pos_j <= pos_i
        for h in range(A_HEADS):
            w = jnp.where(mask, ws_ref[h], 0.0).astype(bf16)
            c0 = h * A_HEAD_DIM
            vh = jnp.concatenate(
                [v[n * GMLP_BLOCK:(n + 1) * GMLP_BLOCK, c0:c0 + A_HEAD_DIM] for n in range(nblk)],
                axis=1)
            sg = jnp.dot(w, vh, preferred_element_type=f32)
            bias = bs_ref[:, c0:c0 + A_HEAD_DIM]
            for n in range(nblk):
                r0 = n * GMLP_BLOCK
                ya = u[r0:r0 + GMLP_BLOCK, c0:c0 + A_HEAD_DIM] * (
                    sg[:, n * A_HEAD_DIM:(n + 1) * A_HEAD_DIM] + bias)
                y_ref[r0:r0 + GMLP_BLOCK, c0:c0 + A_HEAD_DIM] = ya.astype(bf16)

    if do_channel:
        o_ref[...] = _rms(o_ref[...] + ffn, fng_ref[...])


def _block_kernel(tiles_per_seq, n_tiles, *refs):
    n_small = 11
    x_ref = refs[0]
    small = refs[1:1 + n_small]
    w_hbm = refs[1 + n_small:1 + n_small + len(WEIGHTS)]
    o_ref = refs[1 + n_small + len(WEIGHTS)]
    w_vmem = refs[2 + n_small + len(WEIGHTS):2 + n_small + 2 * len(WEIGHTS)]
    hbuf_ref, cbuf_ref, y_ref, xs_ref, stage_ref, sems = refs[2 + n_small + 2 * len(WEIGHTS):]
    t = pl.program_id(0)
    args = (t, x_ref) + tuple(small) + tuple(w_vmem) + (o_ref, hbuf_ref, cbuf_ref, y_ref, xs_ref)

    @pl.when(t % tiles_per_seq == 0)
    def _():
        hbuf_ref[:, 0:HALO, :] = jnp.zeros((B_LANE_GROUPS, HALO, LANES), jnp.float32)

    @pl.when(t == 0)
    def _():
        _load_weights(w_hbm, w_vmem, stage_ref, sems)
        _step(True, False, *args)

    @pl.when(jnp.logical_and(t > 0, t < n_tiles))
    def _():
        _step(True, True, *args)

    @pl.when(t == n_tiles)
    def _():
        _step(False, True, *args)


def _const_spec(shape):
    nd = len(shape)
    return pl.BlockSpec(shape, lambda t: (0,) * nd, pipeline_mode=pl.Buffered(1))


def kernel(x, norm1_g, w_in, gmlp_ln_g, gmlp_ln_b, gmlp_w_s, gmlp_b_s, conv_w, conv_b,
           conv_ln_g, conv_ln_b, w_out, norm2_g, w_gate, w_up, w_down, final_norm_g):
    bsz, seq, d = x.shape
    assert d == D_MODEL and seq % TM == 0
    tiles_per_seq = seq // TM
    n_tiles = bsz * tiles_per_seq
    row = lambda a: a.reshape(1, -1)
    bs_tile = jnp.repeat(gmlp_b_s.T, A_HEAD_DIM, axis=1)
    x2 = x.reshape(bsz * seq, d)
    small = (row(norm1_g), row(gmlp_ln_g), row(gmlp_ln_b), gmlp_w_s, bs_tile, conv_w, row(conv_b),
             row(conv_ln_g), row(conv_ln_b), row(norm2_g), row(final_norm_g))
    weights = (w_in, w_out, w_gate, w_up, w_down)
    for w, (_, rows, cols, _) in zip(weights, WEIGHTS):
        assert w.shape == (rows, cols)
    cur_spec = pl.BlockSpec((TM, D_MODEL), lambda t: (jnp.minimum(t, n_tiles - 1), 0))
    prev_spec = pl.BlockSpec((TM, D_MODEL), lambda t: (jnp.maximum(t - 1, 0), 0))
    in_specs = ([cur_spec] + [_const_spec(a.shape) for a in small]
                + [pl.BlockSpec(memory_space=pl.ANY)] * len(weights))
    out = pl.pallas_call(
        functools.partial(_block_kernel, tiles_per_seq, n_tiles),
        grid=(n_tiles + 1,),
        in_specs=in_specs,
        out_specs=prev_spec,
        out_shape=jax.ShapeDtypeStruct(x2.shape, x.dtype),
        scratch_shapes=(
            [pltpu.VMEM((rows, cols), jnp.bfloat16) for _, rows, cols, _ in WEIGHTS] + [
                pltpu.VMEM((B_LANE_GROUPS, TM + HALO, LANES), jnp.float32),
                pltpu.VMEM((TM, B_WIDTH), jnp.float32),
                pltpu.VMEM((TM, D_MODEL), jnp.bfloat16),
                pltpu.VMEM((TM, D_MODEL), jnp.float32),
                pltpu.VMEM((STAGE_SLOTS, STAGE_ROWS, STAGE_COLS), jnp.float32),
                pltpu.SemaphoreType.DMA((STAGE_SLOTS,)),
            ]),
        compiler_params=pltpu.CompilerParams(
            dimension_semantics=("arbitrary",),
            vmem_limit_bytes=VMEM_LIMIT_BYTES),
        name="hybrid_block",
    )(x2, *small, *weights)
    return out.reshape(x.shape)
```

```python
import functools

import jax
import jax.numpy as jnp
from jax import lax
from jax.experimental import pallas as pl
from jax.experimental.pallas import tpu as pltpu

D_MODEL = 1024
A_WIDTH = 512
B_WIDTH = 512
IN_WIDTH = 2 * A_WIDTH + 2 * B_WIDTH
GMLP_BLOCK = 128
A_HEADS = 4
A_HEAD_DIM = A_WIDTH // A_HEADS
CHUNK = 64
CONV_WIDTH = 31
FFN_HIDDEN = 2816
RMS_EPS = 1e-6
LN_EPS = 1e-5

TM = 512
HALO = 32
PIECE_ROWS = 32
LANES = 128
B_LANE_GROUPS = B_WIDTH // LANES
MXU_TILE = 256
PACKED_ROWS = 16
FFN_CHUNK = 512
STAGE_ROWS = 512
STAGE_COLS = 1024
STAGE_SLOTS = 3
PIECES_PER_CHUNK_LOAD = 4
VMEM_LIMIT_BYTES = 58 * 1024 * 1024

WEIGHTS = (
    ("w_in", D_MODEL, IN_WIDTH, 0.5),
    ("w_out", D_MODEL, D_MODEL, 1.0),
    ("w_gate", D_MODEL, FFN_HIDDEN, 0.5),
    ("w_up", D_MODEL, FFN_HIDDEN, 1.0),
    ("w_down", FFN_HIDDEN, D_MODEL, 1.0),
)


def _rms(x, g):
    ms = jnp.mean(x * x, axis=-1, keepdims=True)
    return x * lax.rsqrt(ms + RMS_EPS) * g


def _ln(x, g, b):
    mu = jnp.mean(x, axis=-1, keepdims=True)
    xc = x - mu
    var = jnp.mean(xc * xc, axis=-1, keepdims=True)
    return xc * lax.rsqrt(var + LN_EPS) * g + b


def _one_plus_tanh(x):
    return jnp.tanh(x) + 1.0


GELU_C = 0.7978845608028654
GELU_K = 0.044715


def _gelu_from_half(xh):
    inner = (2.0 * GELU_C * xh) * (1.0 + (4.0 * GELU_K) * (xh * xh))
    return xh * _one_plus_tanh(inner)


def _ffn_chunks():
    return [(f0, min(f0 + FFN_CHUNK, FFN_HIDDEN)) for f0 in range(0, FFN_HIDDEN, FFN_CHUNK)]


def _sample(x):
    rows, cols = x.shape
    blocks = [x[r:r + PACKED_ROWS, c:c + LANES]
              for r in range(0, rows, PACKED_ROWS) for c in range(0, cols, LANES)]
    while len(blocks) > 1:
        blocks = [blocks[i] + blocks[i + 1] if i + 1 < len(blocks) else blocks[i]
                  for i in range(0, len(blocks), 2)]
    return blocks[0]


def _tie(never, w, pieces):
    by_row = {}
    for (kt, nt), piece in pieces.items():
        by_row.setdefault(kt, {})[nt] = piece
    segs = []
    r = 0
    for kt in sorted(by_row):
        r0 = kt * MXU_TILE
        if r0 > r:
            segs.append(w[r:r0])
        cols = []
        c = 0
        for nt in sorted(by_row[kt]):
            c0 = nt * MXU_TILE
            if c0 > c:
                cols.append(w[r0:r0 + PACKED_ROWS, c:c0])
            cols.append(jnp.where(never, by_row[kt][nt].astype(w.dtype),
                                  w[r0:r0 + PACKED_ROWS, c0:c0 + LANES]))
            c = c0 + LANES
        if c < w.shape[1]:
            cols.append(w[r0:r0 + PACKED_ROWS, c:])
        segs.append(jnp.concatenate(cols, axis=1))
        r = r0 + PACKED_ROWS
    if r < w.shape[0]:
        segs.append(w[r:])
    return jnp.concatenate(segs, axis=0) if len(segs) > 1 else segs[0]


def _weight_chunks():
    chunks = []
    for wi, (_, rows, cols, _) in enumerate(WEIGHTS):
        for r0 in range(0, rows, STAGE_ROWS):
            for c0 in range(0, cols, STAGE_COLS):
                chunks.append((wi, r0, min(STAGE_ROWS, rows - r0), c0, min(STAGE_COLS, cols - c0)))
    return chunks


class _WeightLoader:
    def __init__(self, w_hbm, w_vmem, stage_ref, sems):
        self.w_hbm, self.w_vmem, self.stage_ref, self.sems = w_hbm, w_vmem, stage_ref, sems
        self.chunks = _weight_chunks()
        self.done = 0
        for i in range(min(STAGE_SLOTS - 1, len(self.chunks))):
            self._copy(i).start()

    def _copy(self, i):
        wi, r0, nr, c0, nc = self.chunks[i]
        slot = i % STAGE_SLOTS
        return pltpu.make_async_copy(
            self.w_hbm[wi].at[pl.ds(r0, nr), pl.ds(c0, nc)],
            self.stage_ref.at[slot, pl.ds(0, nr), pl.ds(0, nc)],
            self.sems.at[slot])

    def chunks_of(self, wi):
        return sum(1 for c in self.chunks if c[0] == wi)

    def service(self, n=None):
        n = len(self.chunks) if n is None else n
        for _ in range(n):
            i = self.done
            if i >= len(self.chunks):
                return
            if i + STAGE_SLOTS - 1 < len(self.chunks):
                self._copy(i + STAGE_SLOTS - 1).start()
            self._copy(i).wait()
            wi, r0, nr, c0, nc = self.chunks[i]
            v = self.stage_ref[i % STAGE_SLOTS, 0:nr, 0:nc]
            scale = WEIGHTS[wi][3]
            self.w_vmem[wi][r0:r0 + nr, c0:c0 + nc] = (v if scale == 1.0 else scale * v).astype(jnp.bfloat16)
            self.done += 1


def _step(do_mixer, do_channel, t,
          x_ref, n1g_ref, alng_ref, alnb_ref, ws_ref, bs_ref, cw_ref, cb_ref, clng_ref, clnb_ref,
          n2g_ref, fng_ref, w_in_ref, w_out_ref, w_gate_ref, w_up_ref, w_down_ref,
          o_ref, hbuf_ref, cbuf_ref, y_ref, xs_ref, after_piece=None):
    never = t < 0
    f32 = jnp.float32
    bf16 = jnp.bfloat16
    tie_pieces = do_mixer and do_channel

    if do_channel:
        h1 = xs_ref[...] + jnp.dot(y_ref[...], w_out_ref[...], preferred_element_type=f32)
        o_ref[...] = h1

    queue = iter(())
    if do_mixer:
        x = x_ref[...]
        xs_ref[...] = x
        xn = _rms(x, n1g_ref[...]).astype(bf16)
        z = jnp.dot(xn, w_in_ref[...], preferred_element_type=f32)

    if do_channel:
        hn2 = _rms(h1, n2g_ref[...]).astype(bf16)

    row_blocks = list(range(0, TM, PIECE_ROWS))
    if do_mixer:
        tap0 = HALO - (CONV_WIDTH - 1)

        def conv_piece(g, r0, after=None):
            if r0 == 0:
                zb = z[:, 2 * A_WIDTH + g * LANES:2 * A_WIDTH + (g + 1) * LANES]
                zg = z[:, 2 * A_WIDTH + B_WIDTH + g * LANES:2 * A_WIDTH + B_WIDTH + (g + 1) * LANES]
                hbuf_ref[g, HALO:HALO + TM, :] = zb * _one_plus_tanh(zg)
            acc = jnp.zeros((PIECE_ROWS, LANES), f32)
            if after is not None:
                acc = jnp.where(never, after, acc)
            for k in range(CONV_WIDTH):
                acc = acc + (hbuf_ref[g, r0 + tap0 + k:r0 + tap0 + k + PIECE_ROWS, :]
                             * cw_ref[k:k + 1, g * LANES:(g + 1) * LANES])
            cbuf_ref[r0:r0 + PIECE_ROWS, g * LANES:(g + 1) * LANES] = acc
            if r0 == row_blocks[-1]:
                hbuf_ref[g, 0:HALO, :] = hbuf_ref[g, TM:TM + HALO, :]
            return _sample(acc) if tie_pieces else None

        half_clng = 0.5 * clng_ref[...]
        half_clnb = 0.5 * clnb_ref[...]

        def conv_norm_piece(r0, after=None):
            cv = cbuf_ref[r0:r0 + PIECE_ROWS, :]
            if after is not None:
                cv = jnp.concatenate([jnp.where(never, after, cv[:, 0:LANES]), cv[:, LANES:]], axis=1)
            hn = _ln(cv + cb_ref[...], half_clng, half_clnb)
            yb = hn * _one_plus_tanh(hn)
            y_ref[r0:r0 + PIECE_ROWS, A_WIDTH:] = yb.astype(bf16)
            return _sample(yb) if tie_pieces else None

        u_blocks, v_blocks = {}, {}

        def gelu_piece(r0, after=None):
            zz = z[r0:r0 + PIECE_ROWS, :2 * A_WIDTH]
            if after is not None:
                zz = jnp.concatenate([jnp.where(never, after, zz[:, 0:LANES]), zz[:, LANES:]], axis=1)
            za = _gelu_from_half(zz)
            u_blocks[r0] = za[:, :A_WIDTH]
            vn = _ln(za[:, A_WIDTH:], alng_ref[...], alnb_ref[...])
            v_blocks[r0] = vn.astype(bf16)
            return _sample(za) if tie_pieces else None

        queue = [functools.partial(conv_piece, g, r0) for g in range(B_LANE_GROUPS) for r0 in row_blocks]
        queue += [functools.partial(conv_norm_piece, r0) for r0 in row_blocks]
        queue += [functools.partial(gelu_piece, r0) for r0 in row_blocks]
        queue = iter(queue)

    def tied(w, tiles, result=None):
        pieces = {}
        for i, tile in enumerate(tiles):
            piece = next(queue, None)
            if piece is not None:
                after = None
                if result is not None:
                    r0 = PIECE_ROWS * (i % len(row_blocks))
                    after = result[r0:r0 + PIECE_ROWS, 0:LANES]
                pieces[tile] = piece(after=after)
        return _tie(never, w, pieces) if pieces else w

    if do_channel:
        k_tiles = D_MODEL // MXU_TILE
        chunk_tiles = [(kt, 0) for kt in range(k_tiles)] + [(kt, 1) for kt in range(1, k_tiles, 2)]
        down_tiles = [(kt, nt) for kt in range(1, FFN_HIDDEN // MXU_TILE) for nt in range(0, D_MODEL // MXU_TILE, 2)]
        acts = []
        result = None
        for f0, f1 in _ffn_chunks():
            tiles = [tl for tl in chunk_tiles if (tl[1] + 1) * MXU_TILE <= f1 - f0]
            gate = jnp.dot(hn2, tied(w_gate_ref[:, f0:f1], tiles, result), preferred_element_type=f32)
            up = jnp.dot(hn2, tied(w_up_ref[:, f0:f1], tiles, result), preferred_element_type=f32)
            acts.append((_one_plus_tanh(gate) * (gate * up)).astype(bf16))
            result = gate
        act = jnp.concatenate(acts, axis=1)
        ffn = jnp.dot(act, tied(w_down_ref[...], down_tiles, result), preferred_element_type=f32)
    for i, piece in enumerate(queue):
        piece()
        if after_piece is not None:
            after_piece(i)

    if do_mixer:
        nblk = TM // GMLP_BLOCK
        u = jnp.concatenate([u_blocks[r0] for r0 in row_blocks], axis=0)
        v = jnp.concatenate([v_blocks[r0] for r0 in row_blocks], axis=0)
        pos_i = lax.broadcasted_iota(jnp.int32, (GMLP_BLOCK, GMLP_BLOCK), 0) // CHUNK
        pos_j = lax.broadcasted_iota(jnp.int32, (GMLP_BLOCK, GMLP_BLOCK), 1) // CHUNK
        mask = pos_j <= pos_i
        for h in range(A_HEADS):
            w = jnp.where(mask, ws_ref[h], 0.0).astype(bf16)
            c0 = h * A_HEAD_DIM
            vh = jnp.concatenate(
                [v[n * GMLP_BLOCK:(n + 1) * GMLP_BLOCK, c0:c0 + A_HEAD_DIM] for n in range(nblk)],
                axis=1)
            sg = jnp.dot(w, vh, preferred_element_type=f32)
            bias = bs_ref[:, c0:c0 + A_HEAD_DIM]
            for n in range(nblk):
                r0 = n * GMLP_BLOCK
                ya = u[r0:r0 + GMLP_BLOCK, c0:c0 + A_HEAD_DIM] * (
                    sg[:, n * A_HEAD_DIM:(n + 1) * A_HEAD_DIM] + bias)
                y_ref[r0:r0 + GMLP_BLOCK, c0:c0 + A_HEAD_DIM] = ya.astype(bf16)

    if do_channel:
        o_ref[...] = _rms(o_ref[...] + ffn, fng_ref[...])


def _block_kernel(tiles_per_seq, n_tiles, *refs):
    n_small = 11
    x_ref = refs[0]
    small = refs[1:1 + n_small]
    w_hbm = refs[1 + n_small:1 + n_small + len(WEIGHTS)]
    o_ref = refs[1 + n_small + len(WEIGHTS)]
    w_vmem = refs[2 + n_small + len(WEIGHTS):2 + n_small + 2 * len(WEIGHTS)]
    hbuf_ref, cbuf_ref, y_ref, xs_ref, stage_ref, sems = refs[2 + n_small + 2 * len(WEIGHTS):]
    t = pl.program_id(0)
    args = (t, x_ref) + tuple(small) + tuple(w_vmem) + (o_ref, hbuf_ref, cbuf_ref, y_ref, xs_ref)

    @pl.when(t % tiles_per_seq == 0)
    def _():
        hbuf_ref[:, 0:HALO, :] = jnp.zeros((B_LANE_GROUPS, HALO, LANES), jnp.float32)

    @pl.when(t == 0)
    def _():
        loader = _WeightLoader(w_hbm, w_vmem, stage_ref, sems)
        loader.service(loader.chunks_of(0))
        _step(True, False, *args,
              after_piece=lambda i: loader.service(1) if i % PIECES_PER_CHUNK_LOAD == 0 else None)
        loader.service()

    @pl.when(jnp.logical_and(t > 0, t < n_tiles))
    def _():
        _step(True, True, *args)

    @pl.when(t == n_tiles)
    def _():
        _step(False, True, *args)


def _const_spec(shape):
    nd = len(shape)
    return pl.BlockSpec(shape, lambda t: (0,) * nd, pipeline_mode=pl.Buffered(1))


def kernel(x, norm1_g, w_in, gmlp_ln_g, gmlp_ln_b, gmlp_w_s, gmlp_b_s, conv_w, conv_b,
           conv_ln_g, conv_ln_b, w_out, norm2_g, w_gate, w_up, w_down, final_norm_g):
    bsz, seq, d = x.shape
    assert d == D_MODEL and seq % TM == 0
    tiles_per_seq = seq // TM
    n_tiles = bsz * tiles_per_seq
    row = lambda a: a.reshape(1, -1)
    bs_tile = jnp.repeat(gmlp_b_s.T, A_HEAD_DIM, axis=1)
    x2 = x.reshape(bsz * seq, d)
    small = (row(norm1_g), row(gmlp_ln_g), row(gmlp_ln_b), gmlp_w_s, bs_tile, conv_w, row(conv_b),
             row(conv_ln_g), row(conv_ln_b), row(norm2_g), row(final_norm_g))
    weights = (w_in, w_out, w_gate, w_up, w_down)
    for w, (_, rows, cols, _) in zip(weights, WEIGHTS):
        assert w.shape == (rows, cols)
    cur_spec = pl.BlockSpec((TM, D_MODEL), lambda t: (jnp.minimum(t, n_tiles - 1), 0))
    prev_spec = pl.BlockSpec((TM, D_MODEL), lambda t: (jnp.maximum(t - 1, 0), 0))
    in_specs = ([cur_spec] + [_const_spec(a.shape) for a in small]
                + [pl.BlockSpec(memory_space=pl.ANY)] * len(weights))
    out = pl.pallas_call(
        functools.partial(_block_kernel, tiles_per_seq, n_tiles),
        grid=(n_tiles + 1,),
        in_specs=in_specs,
        out_specs=prev_spec,
        out_shape=jax.ShapeDtypeStruct(x2.shape, x.dtype),
        scratch_shapes=(
            [pltpu.VMEM((rows, cols), jnp.bfloat16) for _, rows, cols, _ in WEIGHTS] + [
                pltpu.VMEM((B_LANE_GROUPS, TM + HALO, LANES), jnp.float32),
                pltpu.VMEM((TM, B_WIDTH), jnp.float32),
                pltpu.VMEM((TM, D_MODEL), jnp.bfloat16),
                pltpu.VMEM((TM, D_MODEL), jnp.float32),
                pltpu.VMEM((STAGE_SLOTS, STAGE_ROWS, STAGE_COLS), jnp.float32),
                pltpu.SemaphoreType.DMA((STAGE_SLOTS,)),
            ]),
        compiler_params=pltpu.CompilerParams(
            dimension_semantics=("arbitrary",),
            vmem_limit_bytes=VMEM_LIMIT_BYTES),
        name="hybrid_block",
    )(x2, *small, *weights)
    return out.reshape(x.shape)
```

```python
import functools

import jax
import jax.numpy as jnp
from jax import lax
from jax.experimental import pallas as pl
from jax.experimental.pallas import tpu as pltpu

D_MODEL = 1024
A_WIDTH = 512
B_WIDTH = 512
IN_WIDTH = 2 * A_WIDTH + 2 * B_WIDTH
GMLP_BLOCK = 128
A_HEADS = 4
A_HEAD_DIM = A_WIDTH // A_HEADS
CHUNK = 64
CONV_WIDTH = 31
FFN_HIDDEN = 2816
RMS_EPS = 1e-6
LN_EPS = 1e-5

TM = 512
HALO = 32
PIECE_ROWS = 32
NORM_ROWS = 64
NORM_TILES = ((2, 2), (3, 2), (2, 3), (3, 3))
DOWN_FIRST_TIED_KT = 2
LANES = 128
B_LANE_GROUPS = B_WIDTH // LANES
MXU_TILE = 256
PACKED_ROWS = 16
FFN_CHUNK = 512
STAGE_ROWS = 256
STAGE_COLS = 1024
STAGE_SLOTS = 3
PIECES_PER_CHUNK_LOAD = 2
VMEM_LIMIT_BYTES = 58 * 1024 * 1024

WEIGHTS = (
    ("w_in", D_MODEL, IN_WIDTH, 0.5),
    ("w_out", D_MODEL, D_MODEL, 1.0),
    ("w_gate", D_MODEL, FFN_HIDDEN, 0.5),
    ("w_up", D_MODEL, FFN_HIDDEN, 1.0),
    ("w_down", FFN_HIDDEN, D_MODEL, 1.0),
)


def _rms(x, g):
    ms = jnp.mean(x * x, axis=-1, keepdims=True)
    return x * lax.rsqrt(ms + RMS_EPS) * g


def _ln(x, g, b):
    mu = jnp.mean(x, axis=-1, keepdims=True)
    xc = x - mu
    var = jnp.mean(xc * xc, axis=-1, keepdims=True)
    return xc * lax.rsqrt(var + LN_EPS) * g + b


def _one_plus_tanh(x):
    return jnp.tanh(x) + 1.0


GELU_C = 0.7978845608028654
GELU_K = 0.044715


def _gelu_from_half(xh):
    inner = (2.0 * GELU_C * xh) * (1.0 + (4.0 * GELU_K) * (xh * xh))
    return xh * _one_plus_tanh(inner)


def _ffn_chunks():
    return [(f0, min(f0 + FFN_CHUNK, FFN_HIDDEN)) for f0 in range(0, FFN_HIDDEN, FFN_CHUNK)]


def _sample(x):
    rows, cols = x.shape
    blocks = [x[r:r + PACKED_ROWS, c:c + LANES]
              for r in range(0, rows, PACKED_ROWS) for c in range(0, cols, LANES)]
    while len(blocks) > 1:
        blocks = [blocks[i] + blocks[i + 1] if i + 1 < len(blocks) else blocks[i]
                  for i in range(0, len(blocks), 2)]
    return blocks[0]


def _tie(never, w, pieces):
    by_row = {}
    for (kt, nt), piece in pieces.items():
        by_row.setdefault(kt, {})[nt] = piece
    segs = []
    r = 0
    for kt in sorted(by_row):
        r0 = kt * MXU_TILE
        if r0 > r:
            segs.append(w[r:r0])
        cols = []
        c = 0
        for nt in sorted(by_row[kt]):
            c0 = nt * MXU_TILE
            if c0 > c:
                cols.append(w[r0:r0 + PACKED_ROWS, c:c0])
            cols.append(jnp.where(never, by_row[kt][nt].astype(w.dtype),
                                  w[r0:r0 + PACKED_ROWS, c0:c0 + LANES]))
            c = c0 + LANES
        if c < w.shape[1]:
            cols.append(w[r0:r0 + PACKED_ROWS, c:])
        segs.append(jnp.concatenate(cols, axis=1))
        r = r0 + PACKED_ROWS
    if r < w.shape[0]:
        segs.append(w[r:])
    return jnp.concatenate(segs, axis=0) if len(segs) > 1 else segs[0]


def _weight_chunks():
    chunks = []
    for wi, (_, rows, cols, _) in enumerate(WEIGHTS):
        for r0 in range(0, rows, STAGE_ROWS):
            for c0 in range(0, cols, STAGE_COLS):
                chunks.append((wi, r0, min(STAGE_ROWS, rows - r0), c0, min(STAGE_COLS, cols - c0)))
    return chunks


class _WeightLoader:
    def __init__(self, w_hbm, w_vmem, stage_ref, sems, which):
        self.w_hbm, self.w_vmem, self.stage_ref, self.sems = w_hbm, w_vmem, stage_ref, sems
        self.chunks = [c for c in _weight_chunks() if c[0] in which]
        self.done = 0
        for i in range(min(STAGE_SLOTS - 1, len(self.chunks))):
            self._copy(i).start()

    def _copy(self, i):
        wi, r0, nr, c0, nc = self.chunks[i]
        slot = i % STAGE_SLOTS
        return pltpu.make_async_copy(
            self.w_hbm[wi].at[pl.ds(r0, nr), pl.ds(c0, nc)],
            self.stage_ref.at[slot, pl.ds(0, nr), pl.ds(0, nc)],
            self.sems.at[slot])

    def service(self, n=None):
        n = len(self.chunks) if n is None else n
        for _ in range(n):
            i = self.done
            if i >= len(self.chunks):
                return
            if i + STAGE_SLOTS - 1 < len(self.chunks):
                self._copy(i + STAGE_SLOTS - 1).start()
            self._copy(i).wait()
            wi, r0, nr, c0, nc = self.chunks[i]
            v = self.stage_ref[i % STAGE_SLOTS, 0:nr, 0:nc]
            scale = WEIGHTS[wi][3]
            self.w_vmem[wi][r0:r0 + nr, c0:c0 + nc] = (v if scale == 1.0 else scale * v).astype(jnp.bfloat16)
            self.done += 1


def _step(do_proj, do_mix, do_channel, t,
          x_ref, n1g_ref, alng_ref, alnb_ref, ws_ref, bs_ref, cw_ref, cb_ref, clng_ref, clnb_ref,
          n2g_ref, fng_ref, w_in_ref, w_out_ref, w_gate_ref, w_up_ref, w_down_ref,
          o_ref, hbuf_ref, cbuf_ref, y_ref, xs_ref, za_ref, zb_ref, after_piece=None):
    never = t < 0
    f32 = jnp.float32
    bf16 = jnp.bfloat16
    tie_pieces = do_mix and do_channel
    slot = t % 2
    prev = 1 - slot
    row_blocks = list(range(0, TM, PIECE_ROWS))
    norm_blocks = list(range(0, TM, NORM_ROWS))

    xn = None
    w_out_v = w_out_ref[...] if do_channel else None
    if do_proj:
        xn_blocks, xn_ties = [], {}
        for i, r0 in enumerate(norm_blocks):
            xb = x_ref[r0:r0 + NORM_ROWS, :]
            xnb = _rms(xb, n1g_ref[...])
            xn_blocks.append(xnb.astype(bf16))
            tile = NORM_TILES[i % len(NORM_TILES)]
            xn_ties[tile] = _sample(xnb) if tile not in xn_ties else xn_ties[tile] + _sample(xnb)
        xn = jnp.concatenate(xn_blocks, axis=0)
        if do_channel:
            w_out_v = _tie(never, w_out_v, xn_ties)

    if do_channel:
        h1 = xs_ref[slot] + jnp.dot(y_ref[...], w_out_v, preferred_element_type=f32)
        o_ref[...] = h1

    if do_proj:
        xs_ref[slot] = x_ref[...]
        za_ref[slot] = jnp.dot(xn, w_in_ref[:, 0:2 * A_WIDTH], preferred_element_type=f32)

    if do_channel:
        hn2 = _rms(h1, n2g_ref[...]).astype(bf16)

    queue = []
    if do_mix:
        tap0 = HALO - (CONV_WIDTH - 1)

        def conv_piece(g, r0, after=None):
            if r0 == 0:
                zb = zb_ref[:, g * LANES:(g + 1) * LANES]
                zg = zb_ref[:, B_WIDTH + g * LANES:B_WIDTH + (g + 1) * LANES]
                hbuf_ref[g, HALO:HALO + TM, :] = zb * _one_plus_tanh(zg)
            acc = jnp.zeros((PIECE_ROWS, LANES), f32)
            if after is not None:
                acc = jnp.where(never, after, acc)
            for k in range(CONV_WIDTH):
                acc = acc + (hbuf_ref[g, r0 + tap0 + k:r0 + tap0 + k + PIECE_ROWS, :]
                             * cw_ref[k:k + 1, g * LANES:(g + 1) * LANES])
            cbuf_ref[r0:r0 + PIECE_ROWS, g * LANES:(g + 1) * LANES] = acc
            if r0 == row_blocks[-1]:
                hbuf_ref[g, 0:HALO, :] = hbuf_ref[g, TM:TM + HALO, :]
            return _sample(acc) if tie_pieces else None

        half_clng = 0.5 * clng_ref[...]
        half_clnb = 0.5 * clnb_ref[...]

        def conv_norm_piece(r0, after=None):
            cv = cbuf_ref[r0:r0 + PIECE_ROWS, :]
            if after is not None:
                cv = jnp.concatenate([jnp.where(never, after, cv[:, 0:LANES]), cv[:, LANES:]], axis=1)
            hn = _ln(cv + cb_ref[...], half_clng, half_clnb)
            yb = hn * _one_plus_tanh(hn)
            y_ref[r0:r0 + PIECE_ROWS, A_WIDTH:] = yb.astype(bf16)
            return _sample(yb) if tie_pieces else None

        u_blocks, v_blocks = {}, {}

        def gelu_piece(r0, after=None):
            zz = za_ref[prev, r0:r0 + PIECE_ROWS, :]
            if after is not None:
                zz = jnp.concatenate([jnp.where(never, after, zz[:, 0:LANES]), zz[:, LANES:]], axis=1)
            za = _gelu_from_half(zz)
            u_blocks[r0] = za[:, :A_WIDTH]
            vn = _ln(za[:, A_WIDTH:], alng_ref[...], alnb_ref[...])
            v_blocks[r0] = vn.astype(bf16)
            return _sample(za) if tie_pieces else None

        queue += [functools.partial(conv_piece, g, r0) for g in range(B_LANE_GROUPS) for r0 in row_blocks]
        queue += [functools.partial(conv_norm_piece, r0) for r0 in row_blocks]
        queue += [functools.partial(gelu_piece, r0) for r0 in row_blocks]

    queue = iter(queue)

    def tied(w, tiles, result=None):
        pieces = {}
        for i, tile in enumerate(tiles):
            piece = next(queue, None)
            if piece is not None:
                after = None
                if result is not None:
                    r0 = PIECE_ROWS * (i % len(row_blocks))
                    after = result[r0:r0 + PIECE_ROWS, 0:LANES]
                smp = piece(after=after)
                if smp is not None:
                    pieces[tile] = smp
        return _tie(never, w, pieces) if pieces else w

    if do_channel:
        k_tiles = D_MODEL // MXU_TILE
        chunk_tiles = [(kt, 0) for kt in range(k_tiles)] + [(kt, 1) for kt in range(1, k_tiles, 2)]
        down_tiles = [(kt, nt) for kt in range(DOWN_FIRST_TIED_KT, FFN_HIDDEN // MXU_TILE) for nt in range(D_MODEL // MXU_TILE)]
        acts = []
        result = None
        for f0, f1 in _ffn_chunks():
            tiles = [tl for tl in chunk_tiles if (tl[1] + 1) * MXU_TILE <= f1 - f0]
            gate = jnp.dot(hn2, tied(w_gate_ref[:, f0:f1], tiles, result), preferred_element_type=f32)
            up = jnp.dot(hn2, tied(w_up_ref[:, f0:f1], tiles, result), preferred_element_type=f32)
            acts.append((_one_plus_tanh(gate) * (gate * up)).astype(bf16))
            result = gate
        act = jnp.concatenate(acts, axis=1)
        ffn = jnp.dot(act, tied(w_down_ref[...], down_tiles, result), preferred_element_type=f32)
    for i, piece in enumerate(queue):
        piece(after=None)
        if after_piece is not None:
            after_piece(i)

    if do_mix:
        nblk = TM // GMLP_BLOCK
        u = jnp.concatenate([u_blocks[r0] for r0 in row_blocks], axis=0)
        v = jnp.concatenate([v_blocks[r0] for r0 in row_blocks], axis=0)
        pos_i = lax.broadcasted_iota(jnp.int32, (GMLP_BLOCK, GMLP_BLOCK), 0) // CHUNK
        pos_j = lax.broadcasted_iota(jnp.int32, (GMLP_BLOCK, GMLP_BLOCK), 1) // CHUNK
        mask = pos_j <= pos_i
        for h in range(A_HEADS):
            w = jnp.where(mask, ws_ref[h], 0.0).astype(bf16)
            c0 = h * A_HEAD_DIM
            vh = jnp.concatenate(
                [v[n * GMLP_BLOCK:(n + 1) * GMLP_BLOCK, c0:c0 + A_HEAD_DIM] for n in range(nblk)],
                axis=1)
            sg = jnp.dot(w, vh, preferred_element_type=f32)
            bias = bs_ref[:, c0:c0 + A_HEAD_DIM]
            for n in range(nblk):
                r0 = n * GMLP_BLOCK
                ya = u[r0:r0 + GMLP_BLOCK, c0:c0 + A_HEAD_DIM] * (
                    sg[:, n * A_HEAD_DIM:(n + 1) * A_HEAD_DIM] + bias)
                y_ref[r0:r0 + GMLP_BLOCK, c0:c0 + A_HEAD_DIM] = ya.astype(bf16)

    if do_proj:
        zb_ref[...] = jnp.dot(xn, w_in_ref[:, 2 * A_WIDTH:], preferred_element_type=f32)

    if do_channel:
        o_ref[...] = _rms(o_ref[...] + ffn, fng_ref[...])


def _block_kernel(tiles_per_seq, n_tiles, *refs):
    n_small = 11
    x_ref = refs[0]
    small = refs[1:1 + n_small]
    w_hbm = refs[1 + n_small:1 + n_small + len(WEIGHTS)]
    o_ref = refs[1 + n_small + len(WEIGHTS)]
    w_vmem = refs[2 + n_small + len(WEIGHTS):2 + n_small + 2 * len(WEIGHTS)]
    hbuf_ref, cbuf_ref, y_ref, xs_ref, za_ref, zb_ref, stage_ref, sems = refs[2 + n_small + 2 * len(WEIGHTS):]
    t = pl.program_id(0)
    args = (t, x_ref) + tuple(small) + tuple(w_vmem) + (o_ref, hbuf_ref, cbuf_ref, y_ref, xs_ref, za_ref, zb_ref)

    @pl.when((t - 1) % tiles_per_seq == 0)
    def _():
        hbuf_ref[:, 0:HALO, :] = jnp.zeros((B_LANE_GROUPS, HALO, LANES), jnp.float32)

    @pl.when(t == 0)
    def _():
        _WeightLoader(w_hbm, w_vmem, stage_ref, sems, which=(0,)).service()
        _step(True, False, False, *args)

    @pl.when(t == 1)
    def _():
        loader = _WeightLoader(w_hbm, w_vmem, stage_ref, sems, which=(1, 2, 3, 4))
        _step(True, True, False, *args,
              after_piece=lambda i: loader.service(1) if i % PIECES_PER_CHUNK_LOAD == 0 else None)
        loader.service()

    @pl.when(jnp.logical_and(t > 1, t <= n_tiles))
    def _():
        _step(True, True, True, *args)

    @pl.when(t == n_tiles + 1)
    def _():
        _step(False, False, True, *args)


def _const_spec(shape):
    nd = len(shape)
    return pl.BlockSpec(shape, lambda t: (0,) * nd, pipeline_mode=pl.Buffered(1))


def kernel(x, norm1_g, w_in, gmlp_ln_g, gmlp_ln_b, gmlp_w_s, gmlp_b_s, conv_w, conv_b,
           conv_ln_g, conv_ln_b, w_out, norm2_g, w_gate, w_up, w_down, final_norm_g):
    bsz, seq, d = x.shape
    assert d == D_MODEL and seq % TM == 0
    tiles_per_seq = seq // TM
    n_tiles = bsz * tiles_per_seq
    row = lambda a: a.reshape(1, -1)
    bs_tile = jnp.repeat(gmlp_b_s.T, A_HEAD_DIM, axis=1)
    x2 = x.reshape(bsz * seq, d)
    small = (row(norm1_g), row(gmlp_ln_g), row(gmlp_ln_b), gmlp_w_s, bs_tile, conv_w, row(conv_b),
             row(conv_ln_g), row(conv_ln_b), row(norm2_g), row(final_norm_g))
    weights = (w_in, w_out, w_gate, w_up, w_down)
    for w, (_, rows, cols, _) in zip(weights, WEIGHTS):
        assert w.shape == (rows, cols)
    cur_spec = pl.BlockSpec((TM, D_MODEL), lambda t: (jnp.minimum(t, n_tiles - 1), 0))
    prev_spec = pl.BlockSpec((TM, D_MODEL), lambda t: (jnp.maximum(t - 2, 0), 0))
    in_specs = ([cur_spec] + [_const_spec(a.shape) for a in small]
                + [pl.BlockSpec(memory_space=pl.ANY)] * len(weights))
    out = pl.pallas_call(
        functools.partial(_block_kernel, tiles_per_seq, n_tiles),
        grid=(n_tiles + 2,),
        in_specs=in_specs,
        out_specs=prev_spec,
        out_shape=jax.ShapeDtypeStruct(x2.shape, x.dtype),
        scratch_shapes=(
            [pltpu.VMEM((rows, cols), jnp.bfloat16) for _, rows, cols, _ in WEIGHTS] + [
                pltpu.VMEM((B_LANE_GROUPS, TM + HALO, LANES), jnp.float32),
                pltpu.VMEM((TM, B_WIDTH), jnp.float32),
                pltpu.VMEM((TM, D_MODEL), jnp.bfloat16),
                pltpu.VMEM((2, TM, D_MODEL), jnp.float32),
                pltpu.VMEM((2, TM, 2 * A_WIDTH), jnp.float32),
                pltpu.VMEM((TM, 2 * B_WIDTH), jnp.float32),
                pltpu.VMEM((STAGE_SLOTS, STAGE_ROWS, STAGE_COLS), jnp.float32),
                pltpu.SemaphoreType.DMA((STAGE_SLOTS,)),
            ]),
        compiler_params=pltpu.CompilerParams(
            dimension_semantics=("arbitrary",),
            vmem_limit_bytes=VMEM_LIMIT_BYTES),
        name="hybrid_block",
    )(x2, *small, *weights)
    return out.reshape(x.shape)
```

```python
import functools

import jax
import jax.numpy as jnp
from jax import lax
from jax.experimental import pallas as pl
from jax.experimental.pallas import tpu as pltpu

D_MODEL = 1024
A_WIDTH = 512
B_WIDTH = 512
IN_WIDTH = 2 * A_WIDTH + 2 * B_WIDTH
GMLP_BLOCK = 128
A_HEADS = 4
A_HEAD_DIM = A_WIDTH // A_HEADS
CHUNK = 64
CONV_WIDTH = 31
FFN_HIDDEN = 2816
RMS_EPS = 1e-6
LN_EPS = 1e-5

TM = 512
HALO = 32
PIECE_ROWS = 32
NORM_ROWS = 64
NORM_TILES = ((2, 2), (3, 2), (2, 3), (3, 3))
DOWN_FIRST_TIED_KT = 2
LANES = 128
B_LANE_GROUPS = B_WIDTH // LANES
MXU_TILE = 256
PACKED_ROWS = 16
FFN_CHUNK = 512
STAGE_ROWS = 512
STAGE_COLS = 1024
STAGE_SLOTS = 3
PIECES_PER_CHUNK_LOAD = 4
VMEM_LIMIT_BYTES = 58 * 1024 * 1024

WEIGHTS = (
    ("w_in", D_MODEL, IN_WIDTH, 0.5),
    ("w_out", D_MODEL, D_MODEL, 1.0),
    ("w_gate", D_MODEL, FFN_HIDDEN, 0.5),
    ("w_up", D_MODEL, FFN_HIDDEN, 1.0),
    ("w_down", FFN_HIDDEN, D_MODEL, 1.0),
)


def _rms(x, g):
    ms = jnp.mean(x * x, axis=-1, keepdims=True)
    return x * lax.rsqrt(ms + RMS_EPS) * g


def _ln(x, g, b):
    mu = jnp.mean(x, axis=-1, keepdims=True)
    xc = x - mu
    var = jnp.mean(xc * xc, axis=-1, keepdims=True)
    return xc * lax.rsqrt(var + LN_EPS) * g + b


def _one_plus_tanh(x):
    return jnp.tanh(x) + 1.0


GELU_C = 0.7978845608028654
GELU_K = 0.044715


def _gelu_from_half(xh):
    inner = (2.0 * GELU_C * xh) * (1.0 + (4.0 * GELU_K) * (xh * xh))
    return xh * _one_plus_tanh(inner)


def _ffn_chunks():
    return [(f0, min(f0 + FFN_CHUNK, FFN_HIDDEN)) for f0 in range(0, FFN_HIDDEN, FFN_CHUNK)]


def _sample(x):
    rows, cols = x.shape
    blocks = [x[r:r + PACKED_ROWS, c:c + LANES]
              for r in range(0, rows, PACKED_ROWS) for c in range(0, cols, LANES)]
    while len(blocks) > 1:
        blocks = [blocks[i] + blocks[i + 1] if i + 1 < len(blocks) else blocks[i]
                  for i in range(0, len(blocks), 2)]
    return blocks[0]


def _tie(never, w, pieces):
    by_row = {}
    for (kt, nt), piece in pieces.items():
        by_row.setdefault(kt, {})[nt] = piece
    segs = []
    r = 0
    for kt in sorted(by_row):
        r0 = kt * MXU_TILE
        if r0 > r:
            segs.append(w[r:r0])
        cols = []
        c = 0
        for nt in sorted(by_row[kt]):
            c0 = nt * MXU_TILE
            if c0 > c:
                cols.append(w[r0:r0 + PACKED_ROWS, c:c0])
            cols.append(jnp.where(never, by_row[kt][nt].astype(w.dtype),
                                  w[r0:r0 + PACKED_ROWS, c0:c0 + LANES]))
            c = c0 + LANES
        if c < w.shape[1]:
            cols.append(w[r0:r0 + PACKED_ROWS, c:])
        segs.append(jnp.concatenate(cols, axis=1))
        r = r0 + PACKED_ROWS
    if r < w.shape[0]:
        segs.append(w[r:])
    return jnp.concatenate(segs, axis=0) if len(segs) > 1 else segs[0]


def _weight_chunks():
    chunks = []
    for wi, (_, rows, cols, _) in enumerate(WEIGHTS):
        for r0 in range(0, rows, STAGE_ROWS):
            for c0 in range(0, cols, STAGE_COLS):
                chunks.append((wi, r0, min(STAGE_ROWS, rows - r0), c0, min(STAGE_COLS, cols - c0)))
    return chunks


class _WeightLoader:
    def __init__(self, w_hbm, w_vmem, slots, sems, which):
        assert len(slots) == STAGE_SLOTS
        self.w_hbm, self.w_vmem, self.slots, self.sems = w_hbm, w_vmem, slots, sems
        self.chunks = [c for c in _weight_chunks() if c[0] in which]
        self.done = 0
        for i in range(min(STAGE_SLOTS - 1, len(self.chunks))):
            self._copy(i).start()

    def _copy(self, i):
        wi, r0, nr, c0, nc = self.chunks[i]
        slot = i % STAGE_SLOTS
        return pltpu.make_async_copy(
            self.w_hbm[wi].at[pl.ds(r0, nr), pl.ds(c0, nc)],
            self.slots[slot].at[pl.ds(0, nr), pl.ds(0, nc)],
            self.sems.at[slot])

    def service(self, n=None):
        n = len(self.chunks) if n is None else n
        for _ in range(n):
            i = self.done
            if i >= len(self.chunks):
                return
            if i + STAGE_SLOTS - 1 < len(self.chunks):
                self._copy(i + STAGE_SLOTS - 1).start()
            self._copy(i).wait()
            wi, r0, nr, c0, nc = self.chunks[i]
            v = self.slots[i % STAGE_SLOTS][0:nr, 0:nc]
            scale = WEIGHTS[wi][3]
            self.w_vmem[wi][r0:r0 + nr, c0:c0 + nc] = (v if scale == 1.0 else scale * v).astype(jnp.bfloat16)
            self.done += 1


def _step(do_proj, do_mix, do_channel, t,
          x_ref, n1g_ref, alng_ref, alnb_ref, ws_ref, bs_ref, cw_ref, cb_ref, clng_ref, clnb_ref,
          n2g_ref, fng_ref, w_in_ref, w_out_ref, w_gate_ref, w_up_ref, w_down_ref,
          o_ref, hbuf_ref, cbuf_ref, y_ref, xs_ref, za_ref, zb_ref, after_piece=None):
    never = t < 0
    f32 = jnp.float32
    bf16 = jnp.bfloat16
    tie_pieces = do_mix and do_channel
    slot = t % 2
    prev = 1 - slot
    row_blocks = list(range(0, TM, PIECE_ROWS))
    norm_blocks = list(range(0, TM, NORM_ROWS))

    xn = None
    w_out_v = w_out_ref[...] if do_channel else None
    if do_proj:
        xn_blocks, xn_ties = [], {}
        for i, r0 in enumerate(norm_blocks):
            xb = x_ref[r0:r0 + NORM_ROWS, :]
            xnb = _rms(xb, n1g_ref[...])
            xn_blocks.append(xnb.astype(bf16))
            tile = NORM_TILES[i % len(NORM_TILES)]
            xn_ties[tile] = _sample(xnb) if tile not in xn_ties else xn_ties[tile] + _sample(xnb)
        xn = jnp.concatenate(xn_blocks, axis=0)
        if do_channel:
            w_out_v = _tie(never, w_out_v, xn_ties)

    if do_channel:
        h1 = xs_ref[slot] + jnp.dot(y_ref[...], w_out_v, preferred_element_type=f32)
        o_ref[...] = h1

    if do_proj:
        xs_ref[slot] = x_ref[...]
        za_ref[slot] = jnp.dot(xn, w_in_ref[:, 0:2 * A_WIDTH], preferred_element_type=f32)

    if do_channel:
        hn2 = _rms(h1, n2g_ref[...]).astype(bf16)

    queue = []
    if do_mix:
        tap0 = HALO - (CONV_WIDTH - 1)

        def conv_piece(g, r0, after=None):
            if r0 == 0:
                zb = zb_ref[:, g * LANES:(g + 1) * LANES]
                zg = zb_ref[:, B_WIDTH + g * LANES:B_WIDTH + (g + 1) * LANES]
                hbuf_ref[g, HALO:HALO + TM, :] = zb * _one_plus_tanh(zg)
            acc = jnp.zeros((PIECE_ROWS, LANES), f32)
            if after is not None:
                acc = jnp.where(never, after, acc)
            for k in range(CONV_WIDTH):
                acc = acc + (hbuf_ref[g, r0 + tap0 + k:r0 + tap0 + k + PIECE_ROWS, :]
                             * cw_ref[k:k + 1, g * LANES:(g + 1) * LANES])
            cbuf_ref[r0:r0 + PIECE_ROWS, g * LANES:(g + 1) * LANES] = acc
            if r0 == row_blocks[-1]:
                hbuf_ref[g, 0:HALO, :] = hbuf_ref[g, TM:TM + HALO, :]
            return _sample(acc) if tie_pieces else None

        half_clng = 0.5 * clng_ref[...]
        half_clnb = 0.5 * clnb_ref[...]

        def conv_norm_piece(r0, after=None):
            cv = cbuf_ref[r0:r0 + PIECE_ROWS, :]
            if after is not None:
                cv = jnp.concatenate([jnp.where(never, after, cv[:, 0:LANES]), cv[:, LANES:]], axis=1)
            hn = _ln(cv + cb_ref[...], half_clng, half_clnb)
            yb = hn * _one_plus_tanh(hn)
            y_ref[r0:r0 + PIECE_ROWS, A_WIDTH:] = yb.astype(bf16)
            return _sample(yb) if tie_pieces else None

        u_blocks, v_blocks = {}, {}

        def gelu_piece(r0, after=None):
            zz = za_ref[prev, r0:r0 + PIECE_ROWS, :]
            if after is not None:
                zz = jnp.concatenate([jnp.where(never, after, zz[:, 0:LANES]), zz[:, LANES:]], axis=1)
            za = _gelu_from_half(zz)
            u_blocks[r0] = za[:, :A_WIDTH]
            vn = _ln(za[:, A_WIDTH:], alng_ref[...], alnb_ref[...])
            v_blocks[r0] = vn.astype(bf16)
            return _sample(za) if tie_pieces else None

        queue += [functools.partial(conv_piece, g, r0) for g in range(B_LANE_GROUPS) for r0 in row_blocks]
        queue += [functools.partial(conv_norm_piece, r0) for r0 in row_blocks]
        queue += [functools.partial(gelu_piece, r0) for r0 in row_blocks]

    queue = iter(queue)

    def tied(w, tiles, result=None):
        pieces = {}
        for i, tile in enumerate(tiles):
            piece = next(queue, None)
            if piece is not None:
                after = None
                if result is not None:
                    r0 = PIECE_ROWS * (i % len(row_blocks))
                    after = result[r0:r0 + PIECE_ROWS, 0:LANES]
                smp = piece(after=after)
                if smp is not None:
                    pieces[tile] = smp
        return _tie(never, w, pieces) if pieces else w

    if do_channel:
        k_tiles = D_MODEL // MXU_TILE
        chunk_tiles = [(kt, 0) for kt in range(k_tiles)] + [(kt, 1) for kt in range(1, k_tiles, 2)]
        down_tiles = [(kt, nt) for kt in range(DOWN_FIRST_TIED_KT, FFN_HIDDEN // MXU_TILE) for nt in range(D_MODEL // MXU_TILE)]
        acts = []
        result = None
        for f0, f1 in _ffn_chunks():
            tiles = [tl for tl in chunk_tiles if (tl[1] + 1) * MXU_TILE <= f1 - f0]
            gate = jnp.dot(hn2, tied(w_gate_ref[:, f0:f1], tiles, result), preferred_element_type=f32)
            up = jnp.dot(hn2, tied(w_up_ref[:, f0:f1], tiles, result), preferred_element_type=f32)
            acts.append((_one_plus_tanh(gate) * (gate * up)).astype(bf16))
            result = gate
        act = jnp.concatenate(acts, axis=1)
        ffn = jnp.dot(act, tied(w_down_ref[...], down_tiles, result), preferred_element_type=f32)
    for i, piece in enumerate(queue):
        piece(after=None)
        if after_piece is not None:
            after_piece(i)

    if do_mix:
        nblk = TM // GMLP_BLOCK
        u = jnp.concatenate([u_blocks[r0] for r0 in row_blocks], axis=0)
        v = jnp.concatenate([v_blocks[r0] for r0 in row_blocks], axis=0)
        pos_i = lax.broadcasted_iota(jnp.int32, (GMLP_BLOCK, GMLP_BLOCK), 0) // CHUNK
        pos_j = lax.broadcasted_iota(jnp.int32, (GMLP_BLOCK, GMLP_BLOCK), 1) // CHUNK
        mask = pos_j <= pos_i
        for h in range(A_HEADS):
            w = jnp.where(mask, ws_ref[h], 0.0).astype(bf16)
            c0 = h * A_HEAD_DIM
            vh = jnp.concatenate(
                [v[n * GMLP_BLOCK:(n + 1) * GMLP_BLOCK, c0:c0 + A_HEAD_DIM] for n in range(nblk)],
                axis=1)
            sg = jnp.dot(w, vh, preferred_element_type=f32)
            bias = bs_ref[:, c0:c0 + A_HEAD_DIM]
            for n in range(nblk):
                r0 = n * GMLP_BLOCK
                ya = u[r0:r0 + GMLP_BLOCK, c0:c0 + A_HEAD_DIM] * (
                    sg[:, n * A_HEAD_DIM:(n + 1) * A_HEAD_DIM] + bias)
                y_ref[r0:r0 + GMLP_BLOCK, c0:c0 + A_HEAD_DIM] = ya.astype(bf16)

    if do_proj:
        zb_ref[...] = jnp.dot(xn, w_in_ref[:, 2 * A_WIDTH:], preferred_element_type=f32)

    if do_channel:
        o_ref[...] = _rms(o_ref[...] + ffn, fng_ref[...])


def _block_kernel(tiles_per_seq, n_tiles, *refs):
    n_small = 11
    x_ref = refs[0]
    small = refs[1:1 + n_small]
    w_hbm = refs[1 + n_small:1 + n_small + len(WEIGHTS)]
    o_ref = refs[1 + n_small + len(WEIGHTS)]
    w_vmem = refs[2 + n_small + len(WEIGHTS):2 + n_small + 2 * len(WEIGHTS)]
    hbuf_ref, cbuf_ref, y_ref, xs_ref, za_ref, zb_ref, stage_ref, sems = refs[2 + n_small + 2 * len(WEIGHTS):]
    slots = [stage_ref.at[i] for i in range(STAGE_SLOTS - 1)] + [o_ref]
    t = pl.program_id(0)
    args = (t, x_ref) + tuple(small) + tuple(w_vmem) + (o_ref, hbuf_ref, cbuf_ref, y_ref, xs_ref, za_ref, zb_ref)

    @pl.when((t - 1) % tiles_per_seq == 0)
    def _():
        hbuf_ref[:, 0:HALO, :] = jnp.zeros((B_LANE_GROUPS, HALO, LANES), jnp.float32)

    @pl.when(t == 0)
    def _():
        _WeightLoader(w_hbm, w_vmem, slots, sems, which=(0,)).service()
        _step(True, False, False, *args)

    @pl.when(t == 1)
    def _():
        loader = _WeightLoader(w_hbm, w_vmem, slots, sems, which=(1, 2, 3, 4))
        _step(True, True, False, *args,
              after_piece=lambda i: loader.service(1) if i % PIECES_PER_CHUNK_LOAD == 0 else None)
        loader.service()

    @pl.when(jnp.logical_and(t > 1, t <= n_tiles))
    def _():
        _step(True, True, True, *args)

    @pl.when(t == n_tiles + 1)
    def _():
        _step(False, False, True, *args)


def _const_spec(shape):
    nd = len(shape)
    return pl.BlockSpec(shape, lambda t: (0,) * nd, pipeline_mode=pl.Buffered(1))


def kernel(x, norm1_g, w_in, gmlp_ln_g, gmlp_ln_b, gmlp_w_s, gmlp_b_s, conv_w, conv_b,
           conv_ln_g, conv_ln_b, w_out, norm2_g, w_gate, w_up, w_down, final_norm_g):
    bsz, seq, d = x.shape
    assert d == D_MODEL and seq % TM == 0
    tiles_per_seq = seq // TM
    n_tiles = bsz * tiles_per_seq
    row = lambda a: a.reshape(1, -1)
    bs_tile = jnp.repeat(gmlp_b_s.T, A_HEAD_DIM, axis=1)
    x2 = x.reshape(bsz * seq, d)
    small = (row(norm1_g), row(gmlp_ln_g), row(gmlp_ln_b), gmlp_w_s, bs_tile, conv_w, row(conv_b),
             row(conv_ln_g), row(conv_ln_b), row(norm2_g), row(final_norm_g))
    weights = (w_in, w_out, w_gate, w_up, w_down)
    for w, (_, rows, cols, _) in zip(weights, WEIGHTS):
        assert w.shape == (rows, cols)
    cur_spec = pl.BlockSpec((TM, D_MODEL), lambda t: (jnp.minimum(t, n_tiles - 1), 0))
    prev_spec = pl.BlockSpec((TM, D_MODEL), lambda t: (jnp.maximum(t - 2, 0), 0))
    in_specs = ([cur_spec] + [_const_spec(a.shape) for a in small]
                + [pl.BlockSpec(memory_space=pl.ANY)] * len(weights))
    out = pl.pallas_call(
        functools.partial(_block_kernel, tiles_per_seq, n_tiles),
        grid=(n_tiles + 2,),
        in_specs=in_specs,
        out_specs=prev_spec,
        out_shape=jax.ShapeDtypeStruct(x2.shape, x.dtype),
        scratch_shapes=(
            [pltpu.VMEM((rows, cols), jnp.bfloat16) for _, rows, cols, _ in WEIGHTS] + [
                pltpu.VMEM((B_LANE_GROUPS, TM + HALO, LANES), jnp.float32),
                pltpu.VMEM((TM, B_WIDTH), jnp.float32),
                pltpu.VMEM((TM, D_MODEL), jnp.bfloat16),
                pltpu.VMEM((2, TM, D_MODEL), jnp.float32),
                pltpu.VMEM((2, TM, 2 * A_WIDTH), jnp.float32),
                pltpu.VMEM((TM, 2 * B_WIDTH), jnp.float32),
                pltpu.VMEM((STAGE_SLOTS - 1, STAGE_ROWS, STAGE_COLS), jnp.float32),
                pltpu.SemaphoreType.DMA((STAGE_SLOTS,)),
            ]),
        compiler_params=pltpu.CompilerParams(
            dimension_semantics=("arbitrary",),
            vmem_limit_bytes=VMEM_LIMIT_BYTES),
        name="hybrid_block",
    )(x2, *small, *weights)
    return out.reshape(x.shape)
```

```python
import functools

import jax
import jax.numpy as jnp
from jax import lax
from jax.experimental import pallas as pl
from jax.experimental.pallas import tpu as pltpu

D_MODEL = 1024
A_WIDTH = 512
B_WIDTH = 512
IN_WIDTH = 2 * A_WIDTH + 2 * B_WIDTH
GMLP_BLOCK = 128
A_HEADS = 4
A_HEAD_DIM = A_WIDTH // A_HEADS
CHUNK = 64
CONV_WIDTH = 31
FFN_HIDDEN = 2816
RMS_EPS = 1e-6
LN_EPS = 1e-5

TM = 512
HALO = 32
PIECE_ROWS = 32
LANES = 128
B_LANE_GROUPS = B_WIDTH // LANES
MXU_TILE = 256
PACKED_ROWS = 16
FFN_CHUNK = 512
STAGE_ROWS = 512
STAGE_COLS = 1024
STAGE_SLOTS = 4
SUB_COPIES = 4
PIECES_PER_CHUNK_LOAD = 4
VMEM_LIMIT_BYTES = 58 * 1024 * 1024

WEIGHTS = (
    ("w_in", D_MODEL, IN_WIDTH, 0.5),
    ("w_out", D_MODEL, D_MODEL, 1.0),
    ("w_gate", D_MODEL, FFN_HIDDEN, 0.5),
    ("w_up", D_MODEL, FFN_HIDDEN, 1.0),
    ("w_down", FFN_HIDDEN, D_MODEL, 1.0),
)


def _rms(x, g):
    ms = jnp.mean(x * x, axis=-1, keepdims=True)
    return x * lax.rsqrt(ms + RMS_EPS) * g


def _ln(x, g, b):
    mu = jnp.mean(x, axis=-1, keepdims=True)
    xc = x - mu
    var = jnp.mean(xc * xc, axis=-1, keepdims=True)
    return xc * lax.rsqrt(var + LN_EPS) * g + b


def _one_plus_tanh(x):
    return jnp.tanh(x) + 1.0


GELU_C = 0.7978845608028654
GELU_K = 0.044715


def _gelu_from_half(xh):
    inner = (2.0 * GELU_C * xh) * (1.0 + (4.0 * GELU_K) * (xh * xh))
    return xh * _one_plus_tanh(inner)


def _ffn_chunks():
    return [(f0, min(f0 + FFN_CHUNK, FFN_HIDDEN)) for f0 in range(0, FFN_HIDDEN, FFN_CHUNK)]


def _sample(x):
    rows, cols = x.shape
    blocks = [x[r:r + PACKED_ROWS, c:c + LANES]
              for r in range(0, rows, PACKED_ROWS) for c in range(0, cols, LANES)]
    while len(blocks) > 1:
        blocks = [blocks[i] + blocks[i + 1] if i + 1 < len(blocks) else blocks[i]
                  for i in range(0, len(blocks), 2)]
    return blocks[0]


def _tie(never, w, pieces):
    by_row = {}
    for (kt, nt), piece in pieces.items():
        by_row.setdefault(kt, {})[nt] = piece
    segs = []
    r = 0
    for kt in sorted(by_row):
        r0 = kt * MXU_TILE
        if r0 > r:
            segs.append(w[r:r0])
        cols = []
        c = 0
        for nt in sorted(by_row[kt]):
            c0 = nt * MXU_TILE
            if c0 > c:
                cols.append(w[r0:r0 + PACKED_ROWS, c:c0])
            cols.append(jnp.where(never, by_row[kt][nt].astype(w.dtype),
                                  w[r0:r0 + PACKED_ROWS, c0:c0 + LANES]))
            c = c0 + LANES
        if c < w.shape[1]:
            cols.append(w[r0:r0 + PACKED_ROWS, c:])
        segs.append(jnp.concatenate(cols, axis=1))
        r = r0 + PACKED_ROWS
    if r < w.shape[0]:
        segs.append(w[r:])
    return jnp.concatenate(segs, axis=0) if len(segs) > 1 else segs[0]


def _weight_chunks():
    chunks = []
    for wi, (_, rows, cols, _) in enumerate(WEIGHTS):
        for r0 in range(0, rows, STAGE_ROWS):
            for c0 in range(0, cols, STAGE_COLS):
                chunks.append((wi, r0, min(STAGE_ROWS, rows - r0), c0, min(STAGE_COLS, cols - c0)))
    return chunks


class _WeightLoader:
    def __init__(self, w_hbm, w_vmem, slots, sems):
        assert len(slots) == STAGE_SLOTS
        self.w_hbm, self.w_vmem, self.slots, self.sems = w_hbm, w_vmem, slots, sems
        self.chunks = _weight_chunks()
        self.done = 0
        for i in range(min(STAGE_SLOTS - 1, len(self.chunks))):
            self._start(i)

    def _copies(self, i):
        wi, r0, nr, c0, nc = self.chunks[i]
        slot = i % STAGE_SLOTS
        sub = nr // SUB_COPIES
        return [pltpu.make_async_copy(
            self.w_hbm[wi].at[pl.ds(r0 + j * sub, sub), pl.ds(c0, nc)],
            self.slots[slot].at[pl.ds(j * sub, sub), pl.ds(0, nc)],
            self.sems.at[slot]) for j in range(SUB_COPIES)]

    def _start(self, i):
        for c in self._copies(i):
            c.start()

    def _wait(self, i):
        for c in self._copies(i):
            c.wait()

    def chunks_of(self, wi):
        return sum(1 for c in self.chunks if c[0] == wi)

    def service(self, n=None):
        n = len(self.chunks) if n is None else n
        for _ in range(n):
            i = self.done
            if i >= len(self.chunks):
                return
            if i + STAGE_SLOTS - 1 < len(self.chunks):
                self._start(i + STAGE_SLOTS - 1)
            self._wait(i)
            wi, r0, nr, c0, nc = self.chunks[i]
            v = self.slots[i % STAGE_SLOTS][0:nr, 0:nc]
            scale = WEIGHTS[wi][3]
            self.w_vmem[wi][r0:r0 + nr, c0:c0 + nc] = (v if scale == 1.0 else scale * v).astype(jnp.bfloat16)
            self.done += 1


def _step(do_mixer, do_channel, t,
          x_ref, n1g_ref, alng_ref, alnb_ref, ws_ref, bs_ref, cw_ref, cb_ref, clng_ref, clnb_ref,
          n2g_ref, fng_ref, w_in_ref, w_out_ref, w_gate_ref, w_up_ref, w_down_ref,
          o_ref, hbuf_ref, cbuf_ref, y_ref, xs_ref, after_piece=None):
    never = t < 0
    f32 = jnp.float32
    bf16 = jnp.bfloat16
    tie_pieces = do_mixer and do_channel

    if do_channel:
        h1 = xs_ref[...] + jnp.dot(y_ref[...], w_out_ref[...], preferred_element_type=f32)
        o_ref[...] = h1

    queue = iter(())
    if do_mixer:
        x = x_ref[...]
        xs_ref[...] = x
        xn = _rms(x, n1g_ref[...]).astype(bf16)
        z = jnp.dot(xn, w_in_ref[...], preferred_element_type=f32)

    if do_channel:
        hn2 = _rms(h1, n2g_ref[...]).astype(bf16)

    row_blocks = list(range(0, TM, PIECE_ROWS))
    if do_mixer:
        tap0 = HALO - (CONV_WIDTH - 1)

        def conv_piece(g, r0, after=None):
            if r0 == 0:
                zb = z[:, 2 * A_WIDTH + g * LANES:2 * A_WIDTH + (g + 1) * LANES]
                zg = z[:, 2 * A_WIDTH + B_WIDTH + g * LANES:2 * A_WIDTH + B_WIDTH + (g + 1) * LANES]
                hbuf_ref[g, HALO:HALO + TM, :] = zb * _one_plus_tanh(zg)
            acc = jnp.zeros((PIECE_ROWS, LANES), f32)
            if after is not None:
                acc = jnp.where(never, after, acc)
            for k in range(CONV_WIDTH):
                acc = acc + (hbuf_ref[g, r0 + tap0 + k:r0 + tap0 + k + PIECE_ROWS, :]
                             * cw_ref[k:k + 1, g * LANES:(g + 1) * LANES])
            cbuf_ref[r0:r0 + PIECE_ROWS, g * LANES:(g + 1) * LANES] = acc
            if r0 == row_blocks[-1]:
                hbuf_ref[g, 0:HALO, :] = hbuf_ref[g, TM:TM + HALO, :]
            return _sample(acc) if tie_pieces else None

        half_clng = 0.5 * clng_ref[...]
        half_clnb = 0.5 * clnb_ref[...]

        def conv_norm_piece(r0, after=None):
            cv = cbuf_ref[r0:r0 + PIECE_ROWS, :]
            if after is not None:
                cv = jnp.concatenate([jnp.where(never, after, cv[:, 0:LANES]), cv[:, LANES:]], axis=1)
            hn = _ln(cv + cb_ref[...], half_clng, half_clnb)
            yb = hn * _one_plus_tanh(hn)
            y_ref[r0:r0 + PIECE_ROWS, A_WIDTH:] = yb.astype(bf16)
            return _sample(yb) if tie_pieces else None

        u_blocks, v_blocks = {}, {}

        def gelu_piece(r0, after=None):
            zz = z[r0:r0 + PIECE_ROWS, :2 * A_WIDTH]
            if after is not None:
                zz = jnp.concatenate([jnp.where(never, after, zz[:, 0:LANES]), zz[:, LANES:]], axis=1)
            za = _gelu_from_half(zz)
            u_blocks[r0] = za[:, :A_WIDTH]
            vn = _ln(za[:, A_WIDTH:], alng_ref[...], alnb_ref[...])
            v_blocks[r0] = vn.astype(bf16)
            return _sample(za) if tie_pieces else None

        queue = [functools.partial(conv_piece, g, r0) for g in range(B_LANE_GROUPS) for r0 in row_blocks]
        queue += [functools.partial(conv_norm_piece, r0) for r0 in row_blocks]
        queue += [functools.partial(gelu_piece, r0) for r0 in row_blocks]
        queue = iter(queue)

    def tied(w, tiles, result=None):
        pieces = {}
        for i, tile in enumerate(tiles):
            piece = next(queue, None)
            if piece is not None:
                after = None
                if result is not None:
                    r0 = PIECE_ROWS * (i % len(row_blocks))
                    after = result[r0:r0 + PIECE_ROWS, 0:LANES]
                pieces[tile] = piece(after=after)
        return _tie(never, w, pieces) if pieces else w

    if do_channel:
        k_tiles = D_MODEL // MXU_TILE
        chunk_tiles = [(kt, 0) for kt in range(k_tiles)] + [(kt, 1) for kt in range(1, k_tiles, 2)]
        down_tiles = [(kt, nt) for kt in range(1, FFN_HIDDEN // MXU_TILE) for nt in range(0, D_MODEL // MXU_TILE, 2)]
        acts = []
        result = None
        for f0, f1 in _ffn_chunks():
            tiles = [tl for tl in chunk_tiles if (tl[1] + 1) * MXU_TILE <= f1 - f0]
            gate = jnp.dot(hn2, tied(w_gate_ref[:, f0:f1], tiles, result), preferred_element_type=f32)
            up = jnp.dot(hn2, tied(w_up_ref[:, f0:f1], tiles, result), preferred_element_type=f32)
            acts.append((_one_plus_tanh(gate) * (gate * up)).astype(bf16))
            result = gate
        act = jnp.concatenate(acts, axis=1)
        ffn = jnp.dot(act, tied(w_down_ref[...], down_tiles, result), preferred_element_type=f32)
    for i, piece in enumerate(queue):
        piece()
        if after_piece is not None:
            after_piece(i)

    if do_mixer:
        nblk = TM // GMLP_BLOCK
        u = jnp.concatenate([u_blocks[r0] for r0 in row_blocks], axis=0)
        v = jnp.concatenate([v_blocks[r0] for r0 in row_blocks], axis=0)
        pos_i = lax.broadcasted_iota(jnp.int32, (GMLP_BLOCK, GMLP_BLOCK), 0) // CHUNK
        pos_j = lax.broadcasted_iota(jnp.int32, (GMLP_BLOCK, GMLP_BLOCK), 1) // CHUNK
        mask = pos_j <= pos_i
        for h in range(A_HEADS):
            w = jnp.where(mask, ws_ref[h], 0.0).astype(bf16)
            c0 = h * A_HEAD_DIM
            vh = jnp.concatenate(
                [v[n * GMLP_BLOCK:(n + 1) * GMLP_BLOCK, c0:c0 + A_HEAD_DIM] for n in range(nblk)],
                axis=1)
            sg = jnp.dot(w, vh, preferred_element_type=f32)
            bias = bs_ref[:, c0:c0 + A_HEAD_DIM]
            for n in range(nblk):
                r0 = n * GMLP_BLOCK
                ya = u[r0:r0 + GMLP_BLOCK, c0:c0 + A_HEAD_DIM] * (
                    sg[:, n * A_HEAD_DIM:(n + 1) * A_HEAD_DIM] + bias)
                y_ref[r0:r0 + GMLP_BLOCK, c0:c0 + A_HEAD_DIM] = ya.astype(bf16)

    if do_channel:
        o_ref[...] = _rms(o_ref[...] + ffn, fng_ref[...])


def _block_kernel(tiles_per_seq, n_tiles, *refs):
    n_small = 11
    x_ref = refs[0]
    small = refs[1:1 + n_small]
    w_hbm = refs[1 + n_small:1 + n_small + len(WEIGHTS)]
    o_ref = refs[1 + n_small + len(WEIGHTS)]
    w_vmem = refs[2 + n_small + len(WEIGHTS):2 + n_small + 2 * len(WEIGHTS)]
    hbuf_ref, cbuf_ref, y_ref, xs_ref, stage_ref, sems = refs[2 + n_small + 2 * len(WEIGHTS):]
    slots = [stage_ref.at[i] for i in range(STAGE_SLOTS - 1)] + [o_ref]
    t = pl.program_id(0)
    args = (t, x_ref) + tuple(small) + tuple(w_vmem) + (o_ref, hbuf_ref, cbuf_ref, y_ref, xs_ref)

    @pl.when(t % tiles_per_seq == 0)
    def _():
        hbuf_ref[:, 0:HALO, :] = jnp.zeros((B_LANE_GROUPS, HALO, LANES), jnp.float32)

    @pl.when(t == 0)
    def _():
        loader = _WeightLoader(w_hbm, w_vmem, slots, sems)
        loader.service(loader.chunks_of(0))
        _step(True, False, *args,
              after_piece=lambda i: loader.service(1) if i % PIECES_PER_CHUNK_LOAD == 0 else None)
        loader.service()

    @pl.when(jnp.logical_and(t > 0, t < n_tiles))
    def _():
        _step(True, True, *args)

    @pl.when(t == n_tiles)
    def _():
        _step(False, True, *args)


def _const_spec(shape):
    nd = len(shape)
    return pl.BlockSpec(shape, lambda t: (0,) * nd, pipeline_mode=pl.Buffered(1))


def kernel(x, norm1_g, w_in, gmlp_ln_g, gmlp_ln_b, gmlp_w_s, gmlp_b_s, conv_w, conv_b,
           conv_ln_g, conv_ln_b, w_out, norm2_g, w_gate, w_up, w_down, final_norm_g):
    bsz, seq, d = x.shape
    assert d == D_MODEL and seq % TM == 0
    tiles_per_seq = seq // TM
    n_tiles = bsz * tiles_per_seq
    row = lambda a: a.reshape(1, -1)
    bs_tile = jnp.repeat(gmlp_b_s.T, A_HEAD_DIM, axis=1)
    x2 = x.reshape(bsz * seq, d)
    small = (row(norm1_g), row(gmlp_ln_g), row(gmlp_ln_b), gmlp_w_s, bs_tile, conv_w, row(conv_b),
             row(conv_ln_g), row(conv_ln_b), row(norm2_g), row(final_norm_g))
    weights = (w_in, w_out, w_gate, w_up, w_down)
    for w, (_, rows, cols, _) in zip(weights, WEIGHTS):
        assert w.shape == (rows, cols)
    cur_spec = pl.BlockSpec((TM, D_MODEL), lambda t: (jnp.minimum(t, n_tiles - 1), 0))
    prev_spec = pl.BlockSpec((TM, D_MODEL), lambda t: (jnp.maximum(t - 1, 0), 0))
    in_specs = ([cur_spec] + [_const_spec(a.shape) for a in small]
                + [pl.BlockSpec(memory_space=pl.ANY)] * len(weights))
    out = pl.pallas_call(
        functools.partial(_block_kernel, tiles_per_seq, n_tiles),
        grid=(n_tiles + 1,),
        in_specs=in_specs,
        out_specs=prev_spec,
        out_shape=jax.ShapeDtypeStruct(x2.shape, x.dtype),
        scratch_shapes=(
            [pltpu.VMEM((rows, cols), jnp.bfloat16) for _, rows, cols, _ in WEIGHTS] + [
                pltpu.VMEM((B_LANE_GROUPS, TM + HALO, LANES), jnp.float32),
                pltpu.VMEM((TM, B_WIDTH), jnp.float32),
                pltpu.VMEM((TM, D_MODEL), jnp.bfloat16),
                pltpu.VMEM((TM, D_MODEL), jnp.float32),
                pltpu.VMEM((STAGE_SLOTS - 1, STAGE_ROWS, STAGE_COLS), jnp.float32),
                pltpu.SemaphoreType.DMA((STAGE_SLOTS,)),
            ]),
        compiler_params=pltpu.CompilerParams(
            dimension_semantics=("arbitrary",),
            vmem_limit_bytes=VMEM_LIMIT_BYTES),
        name="hybrid_block",
    )(x2, *small, *weights)
    return out.reshape(x.shape)
```

```python
import functools

import jax
import jax.numpy as jnp
from jax import lax
from jax.experimental import pallas as pl
from jax.experimental.pallas import tpu as pltpu

D_MODEL = 1024
A_WIDTH = 512
B_WIDTH = 512
IN_WIDTH = 2 * A_WIDTH + 2 * B_WIDTH
GMLP_BLOCK = 128
A_HEADS = 4
A_HEAD_DIM = A_WIDTH // A_HEADS
CHUNK = 64
CONV_WIDTH = 31
FFN_HIDDEN = 2816
RMS_EPS = 1e-6
LN_EPS = 1e-5

TM = 512
HALO = 32
PIECE_ROWS = 32
LANES = 128
B_LANE_GROUPS = B_WIDTH // LANES
MXU_TILE = 256
PACKED_ROWS = 16
FFN_CHUNK = 512
STAGE_ROWS = 512
STAGE_COLS = 1024
STAGE_SLOTS = 4
SUB_COPIES = 4
PIECES_PER_CHUNK_LOAD = 4
VMEM_LIMIT_BYTES = 58 * 1024 * 1024

WEIGHTS = (
    ("w_in", D_MODEL, IN_WIDTH, 0.5),
    ("w_out", D_MODEL, D_MODEL, 1.0),
    ("w_gate", D_MODEL, FFN_HIDDEN, 0.5),
    ("w_up", D_MODEL, FFN_HIDDEN, 1.0),
    ("w_down", FFN_HIDDEN, D_MODEL, 1.0),
)


def _rms(x, g):
    ms = jnp.mean(x * x, axis=-1, keepdims=True)
    return x * lax.rsqrt(ms + RMS_EPS) * g


def _ln(x, g, b):
    mu = jnp.mean(x, axis=-1, keepdims=True)
    xc = x - mu
    var = jnp.mean(xc * xc, axis=-1, keepdims=True)
    return xc * lax.rsqrt(var + LN_EPS) * g + b


def _one_plus_tanh(x):
    return jnp.tanh(x) + 1.0


GELU_C = 0.7978845608028654
GELU_K = 0.044715


def _gelu_from_half(xh):
    inner = (2.0 * GELU_C * xh) * (1.0 + (4.0 * GELU_K) * (xh * xh))
    return xh * _one_plus_tanh(inner)


def _ffn_chunks():
    return [(f0, min(f0 + FFN_CHUNK, FFN_HIDDEN)) for f0 in range(0, FFN_HIDDEN, FFN_CHUNK)]


def _sample(x):
    rows, cols = x.shape
    blocks = [x[r:r + PACKED_ROWS, c:c + LANES]
              for r in range(0, rows, PACKED_ROWS) for c in range(0, cols, LANES)]
    while len(blocks) > 1:
        blocks = [blocks[i] + blocks[i + 1] if i + 1 < len(blocks) else blocks[i]
                  for i in range(0, len(blocks), 2)]
    return blocks[0]


def _tie(never, w, pieces):
    by_row = {}
    for (kt, nt), piece in pieces.items():
        by_row.setdefault(kt, {})[nt] = piece
    segs = []
    r = 0
    for kt in sorted(by_row):
        r0 = kt * MXU_TILE
        if r0 > r:
            segs.append(w[r:r0])
        cols = []
        c = 0
        for nt in sorted(by_row[kt]):
            c0 = nt * MXU_TILE
            if c0 > c:
                cols.append(w[r0:r0 + PACKED_ROWS, c:c0])
            cols.append(jnp.where(never, by_row[kt][nt].astype(w.dtype),
                                  w[r0:r0 + PACKED_ROWS, c0:c0 + LANES]))
            c = c0 + LANES
        if c < w.shape[1]:
            cols.append(w[r0:r0 + PACKED_ROWS, c:])
        segs.append(jnp.concatenate(cols, axis=1))
        r = r0 + PACKED_ROWS
    if r < w.shape[0]:
        segs.append(w[r:])
    return jnp.concatenate(segs, axis=0) if len(segs) > 1 else segs[0]


def _weight_chunks():
    chunks = []
    for wi, (_, rows, cols, _) in enumerate(WEIGHTS):
        for r0 in range(0, rows, STAGE_ROWS):
            for c0 in range(0, cols, STAGE_COLS):
                chunks.append((wi, r0, min(STAGE_ROWS, rows - r0), c0, min(STAGE_COLS, cols - c0)))
    return chunks


class _WeightLoader:
    def __init__(self, w_hbm, w_vmem, slots, sems):
        assert len(slots) == STAGE_SLOTS
        self.w_hbm, self.w_vmem, self.slots, self.sems = w_hbm, w_vmem, slots, sems
        self.chunks = _weight_chunks()
        self.done = 0
        for i in range(min(STAGE_SLOTS - 1, len(self.chunks))):
            self._start(i)

    def _copies(self, i):
        wi, r0, nr, c0, nc = self.chunks[i]
        slot = i % STAGE_SLOTS
        sub = nr // SUB_COPIES
        return [pltpu.make_async_copy(
            self.w_hbm[wi].at[pl.ds(r0 + j * sub, sub), pl.ds(c0, nc)],
            self.slots[slot].at[pl.ds(j * sub, sub), pl.ds(0, nc)],
            self.sems.at[slot]) for j in range(SUB_COPIES)]

    def _start(self, i):
        for c in self._copies(i):
            c.start()

    def _wait(self, i):
        for c in self._copies(i):
            c.wait()

    def chunks_of(self, wi):
        return sum(1 for c in self.chunks if c[0] == wi)

    def service(self, n=None):
        n = len(self.chunks) if n is None else n
        for _ in range(n):
            i = self.done
            if i >= len(self.chunks):
                return
            if i + STAGE_SLOTS - 1 < len(self.chunks):
                self._start(i + STAGE_SLOTS - 1)
            self._wait(i)
            wi, r0, nr, c0, nc = self.chunks[i]
            v = self.slots[i % STAGE_SLOTS][0:nr, 0:nc]
            scale = WEIGHTS[wi][3]
            self.w_vmem[wi][r0:r0 + nr, c0:c0 + nc] = (v if scale == 1.0 else scale * v).astype(jnp.bfloat16)
            self.done += 1


def _step(do_mixer, do_channel, t,
          x_ref, xp_ref, n1g_ref, alng_ref, alnb_ref, ws_ref, bs_ref, cw_ref, cb_ref, clng_ref, clnb_ref,
          n2g_ref, fng_ref, w_in_ref, w_out_ref, w_gate_ref, w_up_ref, w_down_ref,
          o_ref, hbuf_ref, cbuf_ref, y_ref, after_piece=None):
    never = t < 0
    f32 = jnp.float32
    bf16 = jnp.bfloat16
    tie_pieces = do_mixer and do_channel

    if do_channel:
        h1 = xp_ref[...] + jnp.dot(y_ref[...], w_out_ref[...], preferred_element_type=f32)
        o_ref[...] = h1

    queue = iter(())
    if do_mixer:
        xn = _rms(x_ref[...], n1g_ref[...]).astype(bf16)
        z = jnp.dot(xn, w_in_ref[...], preferred_element_type=f32)

    if do_channel:
        hn2 = _rms(h1, n2g_ref[...]).astype(bf16)

    row_blocks = list(range(0, TM, PIECE_ROWS))
    if do_mixer:
        tap0 = HALO - (CONV_WIDTH - 1)

        def conv_piece(g, r0, after=None):
            if r0 == 0:
                zb = z[:, 2 * A_WIDTH + g * LANES:2 * A_WIDTH + (g + 1) * LANES]
                zg = z[:, 2 * A_WIDTH + B_WIDTH + g * LANES:2 * A_WIDTH + B_WIDTH + (g + 1) * LANES]
                hbuf_ref[g, HALO:HALO + TM, :] = zb * _one_plus_tanh(zg)
            acc = jnp.zeros((PIECE_ROWS, LANES), f32)
            if after is not None:
                acc = jnp.where(never, after, acc)
            for k in range(CONV_WIDTH):
                acc = acc + (hbuf_ref[g, r0 + tap0 + k:r0 + tap0 + k + PIECE_ROWS, :]
                             * cw_ref[k:k + 1, g * LANES:(g + 1) * LANES])
            cbuf_ref[r0:r0 + PIECE_ROWS, g * LANES:(g + 1) * LANES] = acc
            if r0 == row_blocks[-1]:
                hbuf_ref[g, 0:HALO, :] = hbuf_ref[g, TM:TM + HALO, :]
            return _sample(acc) if tie_pieces else None

        half_clng = 0.5 * clng_ref[...]
        half_clnb = 0.5 * clnb_ref[...]

        def conv_norm_piece(r0, after=None):
            cv = cbuf_ref[r0:r0 + PIECE_ROWS, :]
            if after is not None:
                cv = jnp.concatenate([jnp.where(never, after, cv[:, 0:LANES]), cv[:, LANES:]], axis=1)
            hn = _ln(cv + cb_ref[...], half_clng, half_clnb)
            yb = hn * _one_plus_tanh(hn)
            y_ref[r0:r0 + PIECE_ROWS, A_WIDTH:] = yb.astype(bf16)
            return _sample(yb) if tie_pieces else None

        u_blocks, v_blocks = {}, {}

        def gelu_piece(r0, after=None):
            zz = z[r0:r0 + PIECE_ROWS, :2 * A_WIDTH]
            if after is not None:
                zz = jnp.concatenate([jnp.where(never, after, zz[:, 0:LANES]), zz[:, LANES:]], axis=1)
            za = _gelu_from_half(zz)
            u_blocks[r0] = za[:, :A_WIDTH]
            vn = _ln(za[:, A_WIDTH:], alng_ref[...], alnb_ref[...])
            v_blocks[r0] = vn.astype(bf16)
            return _sample(za) if tie_pieces else None

        queue = [functools.partial(conv_piece, g, r0) for g in range(B_LANE_GROUPS) for r0 in row_blocks]
        queue += [functools.partial(conv_norm_piece, r0) for r0 in row_blocks]
        queue += [functools.partial(gelu_piece, r0) for r0 in row_blocks]
        queue = iter(queue)

    def tied(w, tiles, result=None):
        pieces = {}
        for i, tile in enumerate(tiles):
            piece = next(queue, None)
            if piece is not None:
                after = None
                if result is not None:
                    r0 = PIECE_ROWS * (i % len(row_blocks))
                    after = result[r0:r0 + PIECE_ROWS, 0:LANES]
                pieces[tile] = piece(after=after)
        return _tie(never, w, pieces) if pieces else w

    if do_channel:
        k_tiles = D_MODEL // MXU_TILE
        chunk_tiles = [(kt, 0) for kt in range(k_tiles)] + [(kt, 1) for kt in range(1, k_tiles, 2)]
        down_tiles = [(kt, nt) for kt in range(1, FFN_HIDDEN // MXU_TILE) for nt in range(0, D_MODEL // MXU_TILE, 2)]
        acts = []
        result = None
        for f0, f1 in _ffn_chunks():
            tiles = [tl for tl in chunk_tiles if (tl[1] + 1) * MXU_TILE <= f1 - f0]
            gate = jnp.dot(hn2, tied(w_gate_ref[:, f0:f1], tiles, result), preferred_element_type=f32)
            up = jnp.dot(hn2, tied(w_up_ref[:, f0:f1], tiles, result), preferred_element_type=f32)
            acts.append((_one_plus_tanh(gate) * (gate * up)).astype(bf16))
            result = gate
        act = jnp.concatenate(acts, axis=1)
        ffn = jnp.dot(act, tied(w_down_ref[...], down_tiles, result), preferred_element_type=f32)
    for i, piece in enumerate(queue):
        piece()
        if after_piece is not None:
            after_piece(i)

    if do_mixer:
        nblk = TM // GMLP_BLOCK
        u = jnp.concatenate([u_blocks[r0] for r0 in row_blocks], axis=0)
        v = jnp.concatenate([v_blocks[r0] for r0 in row_blocks], axis=0)
        pos_i = lax.broadcasted_iota(jnp.int32, (GMLP_BLOCK, GMLP_BLOCK), 0) // CHUNK
        pos_j = lax.broadcasted_iota(jnp.int32, (GMLP_BLOCK, GMLP_BLOCK), 1) // CHUNK
        mask = pos_j <= pos_i
        for h in range(A_HEADS):
            w = jnp.where(mask, ws_ref[h], 0.0).astype(bf16)
            c0 = h * A_HEAD_DIM
            vh = jnp.concatenate(
                [v[n * GMLP_BLOCK:(n + 1) * GMLP_BLOCK, c0:c0 + A_HEAD_DIM] for n in range(nblk)],
                axis=1)
            sg = jnp.dot(w, vh, preferred_element_type=f32)
            bias = bs_ref[:, c0:c0 + A_HEAD_DIM]
            for n in range(nblk):
                r0 = n * GMLP_BLOCK
                ya = u[r0:r0 + GMLP_BLOCK, c0:c0 + A_HEAD_DIM] * (
                    sg[:, n * A_HEAD_DIM:(n + 1) * A_HEAD_DIM] + bias)
                y_ref[r0:r0 + GMLP_BLOCK, c0:c0 + A_HEAD_DIM] = ya.astype(bf16)

    if do_channel:
        o_ref[...] = _rms(o_ref[...] + ffn, fng_ref[...])


def _block_kernel(tiles_per_seq, n_tiles, *refs):
    n_small = 11
    x_ref, xp_ref = refs[0:2]
    small = refs[2:2 + n_small]
    w_hbm = refs[2 + n_small:2 + n_small + len(WEIGHTS)]
    o_ref = refs[2 + n_small + len(WEIGHTS)]
    w_vmem = refs[3 + n_small + len(WEIGHTS):3 + n_small + 2 * len(WEIGHTS)]
    hbuf_ref, cbuf_ref, y_ref, stage_ref, sems = refs[3 + n_small + 2 * len(WEIGHTS):]
    slots = [stage_ref.at[i] for i in range(STAGE_SLOTS - 1)] + [o_ref]
    t = pl.program_id(0)
    args = (t, x_ref, xp_ref) + tuple(small) + tuple(w_vmem) + (o_ref, hbuf_ref, cbuf_ref, y_ref)

    @pl.when(t % tiles_per_seq == 0)
    def _():
        hbuf_ref[:, 0:HALO, :] = jnp.zeros((B_LANE_GROUPS, HALO, LANES), jnp.float32)

    @pl.when(t == 0)
    def _():
        loader = _WeightLoader(w_hbm, w_vmem, slots, sems)
        loader.service(loader.chunks_of(0))
        _step(True, False, *args,
              after_piece=lambda i: loader.service(1) if i % PIECES_PER_CHUNK_LOAD == 0 else None)
        loader.service()

    @pl.when(jnp.logical_and(t > 0, t < n_tiles))
    def _():
        _step(True, True, *args)

    @pl.when(t == n_tiles)
    def _():
        _step(False, True, *args)


def _const_spec(shape):
    nd = len(shape)
    return pl.BlockSpec(shape, lambda t: (0,) * nd, pipeline_mode=pl.Buffered(1))


def kernel(x, norm1_g, w_in, gmlp_ln_g, gmlp_ln_b, gmlp_w_s, gmlp_b_s, conv_w, conv_b,
           conv_ln_g, conv_ln_b, w_out, norm2_g, w_gate, w_up, w_down, final_norm_g):
    bsz, seq, d = x.shape
    assert d == D_MODEL and seq % TM == 0
    tiles_per_seq = seq // TM
    n_tiles = bsz * tiles_per_seq
    row = lambda a: a.reshape(1, -1)
    bs_tile = jnp.repeat(gmlp_b_s.T, A_HEAD_DIM, axis=1)
    x2 = x.reshape(bsz * seq, d)
    small = (row(norm1_g), row(gmlp_ln_g), row(gmlp_ln_b), gmlp_w_s, bs_tile, conv_w, row(conv_b),
             row(conv_ln_g), row(conv_ln_b), row(norm2_g), row(final_norm_g))
    weights = (w_in, w_out, w_gate, w_up, w_down)
    for w, (_, rows, cols, _) in zip(weights, WEIGHTS):
        assert w.shape == (rows, cols)
    cur_spec = pl.BlockSpec((TM, D_MODEL), lambda t: (jnp.minimum(t, n_tiles - 1), 0))
    prev_spec = pl.BlockSpec((TM, D_MODEL), lambda t: (jnp.maximum(t - 1, 0), 0))
    in_specs = ([cur_spec, prev_spec] + [_const_spec(a.shape) for a in small]
                + [pl.BlockSpec(memory_space=pl.ANY)] * len(weights))
    out = pl.pallas_call(
        functools.partial(_block_kernel, tiles_per_seq, n_tiles),
        grid=(n_tiles + 1,),
        in_specs=in_specs,
        out_specs=prev_spec,
        out_shape=jax.ShapeDtypeStruct(x2.shape, x.dtype),
        scratch_shapes=(
            [pltpu.VMEM((rows, cols), jnp.bfloat16) for _, rows, cols, _ in WEIGHTS] + [
                pltpu.VMEM((B_LANE_GROUPS, TM + HALO, LANES), jnp.float32),
                pltpu.VMEM((TM, B_WIDTH), jnp.float32),
                pltpu.VMEM((TM, D_MODEL), jnp.bfloat16),
                pltpu.VMEM((STAGE_SLOTS - 1, STAGE_ROWS, STAGE_COLS), jnp.float32),
                pltpu.SemaphoreType.DMA((STAGE_SLOTS,)),
            ]),
        compiler_params=pltpu.CompilerParams(
            dimension_semantics=("arbitrary",),
            vmem_limit_bytes=VMEM_LIMIT_BYTES),
        name="hybrid_block",
    )(x2, x2, *small, *weights)
    return out.reshape(x.shape)
```

```python
import functools

import jax
import jax.numpy as jnp
from jax import lax
from jax.experimental import pallas as pl
from jax.experimental.pallas import tpu as pltpu

D_MODEL = 1024
A_WIDTH = 512
B_WIDTH = 512
IN_WIDTH = 2 * A_WIDTH + 2 * B_WIDTH
GMLP_BLOCK = 128
A_HEADS = 4
A_HEAD_DIM = A_WIDTH // A_HEADS
CHUNK = 64
CONV_WIDTH = 31
FFN_HIDDEN = 2816
RMS_EPS = 1e-6
LN_EPS = 1e-5

TM = 512
LAG = 2
HALO = 32
PIECE_ROWS = 32
LANES = 128
B_LANE_GROUPS = B_WIDTH // LANES
MXU_TILE = 256
PACKED_ROWS = 16
FFN_CHUNK = 512
STAGE_ROWS = 512
STAGE_COLS = 1024
STAGE_SLOTS = 4
SUB_COPIES = 4
PIECES_PER_CHUNK_LOAD = 8
VMEM_LIMIT_BYTES = 58 * 1024 * 1024

WEIGHTS = (
    ("w_in", D_MODEL, IN_WIDTH, 0.5),
    ("w_out", D_MODEL, D_MODEL, 1.0),
    ("w_gate", D_MODEL, FFN_HIDDEN, 0.5),
    ("w_up", D_MODEL, FFN_HIDDEN, 1.0),
    ("w_down", FFN_HIDDEN, D_MODEL, 1.0),
)


def _rms(x, g):
    ms = jnp.mean(x * x, axis=-1, keepdims=True)
    return x * lax.rsqrt(ms + RMS_EPS) * g


def _ln(x, g, b):
    mu = jnp.mean(x, axis=-1, keepdims=True)
    xc = x - mu
    var = jnp.mean(xc * xc, axis=-1, keepdims=True)
    return xc * lax.rsqrt(var + LN_EPS) * g + b


def _one_plus_tanh(x):
    return jnp.tanh(x) + 1.0


GELU_C = 0.7978845608028654
GELU_K = 0.044715


def _gelu_from_half(xh):
    inner = (2.0 * GELU_C * xh) * (1.0 + (4.0 * GELU_K) * (xh * xh))
    return xh * _one_plus_tanh(inner)


def _ffn_chunks():
    return [(f0, min(f0 + FFN_CHUNK, FFN_HIDDEN)) for f0 in range(0, FFN_HIDDEN, FFN_CHUNK)]


def _sample(x):
    rows, cols = x.shape
    blocks = [x[r:r + PACKED_ROWS, c:c + LANES]
              for r in range(0, rows, PACKED_ROWS) for c in range(0, cols, LANES)]
    while len(blocks) > 1:
        blocks = [blocks[i] + blocks[i + 1] if i + 1 < len(blocks) else blocks[i]
                  for i in range(0, len(blocks), 2)]
    return blocks[0]


def _tie(never, w, pieces):
    by_row = {}
    for (kt, nt), piece in pieces.items():
        by_row.setdefault(kt, {})[nt] = piece
    segs = []
    r = 0
    for kt in sorted(by_row):
        r0 = kt * MXU_TILE
        if r0 > r:
            segs.append(w[r:r0])
        cols = []
        c = 0
        for nt in sorted(by_row[kt]):
            c0 = nt * MXU_TILE
            if c0 > c:
                cols.append(w[r0:r0 + PACKED_ROWS, c:c0])
            cols.append(jnp.where(never, by_row[kt][nt].astype(w.dtype),
                                  w[r0:r0 + PACKED_ROWS, c0:c0 + LANES]))
            c = c0 + LANES
        if c < w.shape[1]:
            cols.append(w[r0:r0 + PACKED_ROWS, c:])
        segs.append(jnp.concatenate(cols, axis=1))
        r = r0 + PACKED_ROWS
    if r < w.shape[0]:
        segs.append(w[r:])
    return jnp.concatenate(segs, axis=0) if len(segs) > 1 else segs[0]


def _weight_chunks():
    chunks = []
    for wi, (_, rows, cols, _) in enumerate(WEIGHTS):
        for r0 in range(0, rows, STAGE_ROWS):
            for c0 in range(0, cols, STAGE_COLS):
                chunks.append((wi, r0, min(STAGE_ROWS, rows - r0), c0, min(STAGE_COLS, cols - c0)))
    return chunks


class _WeightLoader:
    def __init__(self, w_hbm, w_vmem, slots, sems):
        assert len(slots) == STAGE_SLOTS
        self.w_hbm, self.w_vmem, self.slots, self.sems = w_hbm, w_vmem, slots, sems
        self.chunks = _weight_chunks()
        self.done = 0

    def prime(self):
        for i in range(min(STAGE_SLOTS - 1, len(self.chunks))):
            self._start(i)

    def _copies(self, i):
        wi, r0, nr, c0, nc = self.chunks[i]
        slot = i % STAGE_SLOTS
        sub = nr // SUB_COPIES
        return [pltpu.make_async_copy(
            self.w_hbm[wi].at[pl.ds(r0 + j * sub, sub), pl.ds(c0, nc)],
            self.slots[slot].at[pl.ds(j * sub, sub), pl.ds(0, nc)],
            self.sems.at[slot]) for j in range(SUB_COPIES)]

    def _start(self, i):
        for c in self._copies(i):
            c.start()

    def _wait(self, i):
        for c in self._copies(i):
            c.wait()

    def chunks_of(self, wi):
        return sum(1 for c in self.chunks if c[0] == wi)

    def service(self, n=None):
        n = len(self.chunks) if n is None else n
        for _ in range(n):
            i = self.done
            if i >= len(self.chunks):
                return
            if i + STAGE_SLOTS - 1 < len(self.chunks):
                self._start(i + STAGE_SLOTS - 1)
            self._wait(i)
            wi, r0, nr, c0, nc = self.chunks[i]
            v = self.slots[i % STAGE_SLOTS][0:nr, 0:nc]
            scale = WEIGHTS[wi][3]
            self.w_vmem[wi][r0:r0 + nr, c0:c0 + nc] = (v if scale == 1.0 else scale * v).astype(jnp.bfloat16)
            self.done += 1


def _step(do_mixer, do_channel, t,
          x_ref, xp_ref, n1g_ref, alng_ref, alnb_ref, ws_ref, bs_ref, cw_ref, cb_ref, clng_ref, clnb_ref,
          n2g_ref, fng_ref, w_in_ref, w_out_ref, w_gate_ref, w_up_ref, w_down_ref,
          o_ref, hbuf_ref, cbuf_ref, y_ref, after_piece=None):
    never = t < 0
    f32 = jnp.float32
    bf16 = jnp.bfloat16
    tie_pieces = do_mixer and do_channel
    slot = t % LAG

    if do_channel:
        h1 = xp_ref[...] + jnp.dot(y_ref[slot], w_out_ref[...], preferred_element_type=f32)
        o_ref[...] = h1

    queue = iter(())
    if do_mixer:
        xn = _rms(x_ref[...], n1g_ref[...]).astype(bf16)
        z = jnp.dot(xn, w_in_ref[...], preferred_element_type=f32)

    if do_channel:
        hn2 = _rms(h1, n2g_ref[...]).astype(bf16)

    row_blocks = list(range(0, TM, PIECE_ROWS))
    if do_mixer:
        tap0 = HALO - (CONV_WIDTH - 1)

        def conv_piece(g, r0, after=None):
            if r0 == 0:
                zb = z[:, 2 * A_WIDTH + g * LANES:2 * A_WIDTH + (g + 1) * LANES]
                zg = z[:, 2 * A_WIDTH + B_WIDTH + g * LANES:2 * A_WIDTH + B_WIDTH + (g + 1) * LANES]
                hbuf_ref[g, HALO:HALO + TM, :] = zb * _one_plus_tanh(zg)
            acc = jnp.zeros((PIECE_ROWS, LANES), f32)
            if after is not None:
                acc = jnp.where(never, after, acc)
            for k in range(CONV_WIDTH):
                acc = acc + (hbuf_ref[g, r0 + tap0 + k:r0 + tap0 + k + PIECE_ROWS, :]
                             * cw_ref[k:k + 1, g * LANES:(g + 1) * LANES])
            cbuf_ref[r0:r0 + PIECE_ROWS, g * LANES:(g + 1) * LANES] = acc
            if r0 == row_blocks[-1]:
                hbuf_ref[g, 0:HALO, :] = hbuf_ref[g, TM:TM + HALO, :]
            return _sample(acc) if tie_pieces else None

        half_clng = 0.5 * clng_ref[...]
        half_clnb = 0.5 * clnb_ref[...]

        def conv_norm_piece(r0, after=None):
            cv = cbuf_ref[r0:r0 + PIECE_ROWS, :]
            if after is not None:
                cv = jnp.concatenate([jnp.where(never, after, cv[:, 0:LANES]), cv[:, LANES:]], axis=1)
            hn = _ln(cv + cb_ref[...], half_clng, half_clnb)
            yb = hn * _one_plus_tanh(hn)
            y_ref[slot, r0:r0 + PIECE_ROWS, A_WIDTH:] = yb.astype(bf16)
            return _sample(yb) if tie_pieces else None

        u_blocks, v_blocks = {}, {}

        def gelu_piece(r0, after=None):
            zz = z[r0:r0 + PIECE_ROWS, :2 * A_WIDTH]
            if after is not None:
                zz = jnp.concatenate([jnp.where(never, after, zz[:, 0:LANES]), zz[:, LANES:]], axis=1)
            za = _gelu_from_half(zz)
            u_blocks[r0] = za[:, :A_WIDTH]
            vn = _ln(za[:, A_WIDTH:], alng_ref[...], alnb_ref[...])
            v_blocks[r0] = vn.astype(bf16)
            return _sample(za) if tie_pieces else None

        queue = [functools.partial(conv_piece, g, r0) for g in range(B_LANE_GROUPS) for r0 in row_blocks]
        queue += [functools.partial(conv_norm_piece, r0) for r0 in row_blocks]
        queue += [functools.partial(gelu_piece, r0) for r0 in row_blocks]
        queue = iter(queue)

    def tied(w, tiles, result=None):
        pieces = {}
        for i, tile in enumerate(tiles):
            piece = next(queue, None)
            if piece is not None:
                after = None
                if result is not None:
                    r0 = PIECE_ROWS * (i % len(row_blocks))
                    after = result[r0:r0 + PIECE_ROWS, 0:LANES]
                pieces[tile] = piece(after=after)
        return _tie(never, w, pieces) if pieces else w

    if do_channel:
        k_tiles = D_MODEL // MXU_TILE
        chunk_tiles = [(kt, 0) for kt in range(k_tiles)] + [(kt, 1) for kt in range(1, k_tiles, 2)]
        down_tiles = [(kt, nt) for kt in range(1, FFN_HIDDEN // MXU_TILE) for nt in range(0, D_MODEL // MXU_TILE, 2)]
        acts = []
        result = None
        for f0, f1 in _ffn_chunks():
            tiles = [tl for tl in chunk_tiles if (tl[1] + 1) * MXU_TILE <= f1 - f0]
            gate = jnp.dot(hn2, tied(w_gate_ref[:, f0:f1], tiles, result), preferred_element_type=f32)
            up = jnp.dot(hn2, tied(w_up_ref[:, f0:f1], tiles, result), preferred_element_type=f32)
            acts.append((_one_plus_tanh(gate) * (gate * up)).astype(bf16))
            result = gate
        act = jnp.concatenate(acts, axis=1)
        ffn = jnp.dot(act, tied(w_down_ref[...], down_tiles, result), preferred_element_type=f32)
    for i, piece in enumerate(queue):
        piece()
        if after_piece is not None:
            after_piece(i)

    if do_mixer:
        nblk = TM // GMLP_BLOCK
        u = jnp.concatenate([u_blocks[r0] for r0 in row_blocks], axis=0)
        v = jnp.concatenate([v_blocks[r0] for r0 in row_blocks], axis=0)
        pos_i = lax.broadcasted_iota(jnp.int32, (GMLP_BLOCK, GMLP_BLOCK), 0) // CHUNK
        pos_j = lax.broadcasted_iota(jnp.int32, (GMLP_BLOCK, GMLP_BLOCK), 1) // CHUNK
        mask = pos_j <= pos_i
        for h in range(A_HEADS):
            w = jnp.where(mask, ws_ref[h], 0.0).astype(bf16)
            c0 = h * A_HEAD_DIM
            vh = jnp.concatenate(
                [v[n * GMLP_BLOCK:(n + 1) * GMLP_BLOCK, c0:c0 + A_HEAD_DIM] for n in range(nblk)],
                axis=1)
            sg = jnp.dot(w, vh, preferred_element_type=f32)
            bias = bs_ref[:, c0:c0 + A_HEAD_DIM]
            for n in range(nblk):
                r0 = n * GMLP_BLOCK
                ya = u[r0:r0 + GMLP_BLOCK, c0:c0 + A_HEAD_DIM] * (
                    sg[:, n * A_HEAD_DIM:(n + 1) * A_HEAD_DIM] + bias)
                y_ref[slot, r0:r0 + GMLP_BLOCK, c0:c0 + A_HEAD_DIM] = ya.astype(bf16)

    if do_channel:
        o_ref[...] = _rms(o_ref[...] + ffn, fng_ref[...])


def _block_kernel(tiles_per_seq, n_tiles, *refs):
    n_small = 11
    x_ref, xp_ref = refs[0:2]
    small = refs[2:2 + n_small]
    w_hbm = refs[2 + n_small:2 + n_small + len(WEIGHTS)]
    o_ref = refs[2 + n_small + len(WEIGHTS)]
    w_vmem = refs[3 + n_small + len(WEIGHTS):3 + n_small + 2 * len(WEIGHTS)]
    hbuf_ref, cbuf_ref, y_ref, stage_ref, sems = refs[3 + n_small + 2 * len(WEIGHTS):]
    slots = [stage_ref.at[i] for i in range(STAGE_SLOTS - 1)] + [o_ref]
    t = pl.program_id(0)
    args = (t, x_ref, xp_ref) + tuple(small) + tuple(w_vmem) + (o_ref, hbuf_ref, cbuf_ref, y_ref)

    @pl.when(t % tiles_per_seq == 0)
    def _():
        hbuf_ref[:, 0:HALO, :] = jnp.zeros((B_LANE_GROUPS, HALO, LANES), jnp.float32)

    loader = _WeightLoader(w_hbm, w_vmem, slots, sems)
    for step in range(LAG):
        @pl.when(t == step)
        def _(step=step):
            if step == 0:
                loader.prime()
                loader.service(loader.chunks_of(0))
            _step(True, False, *args,
                  after_piece=lambda i: loader.service(1) if i % PIECES_PER_CHUNK_LOAD == 0 else None)
            if step == LAG - 1:
                loader.service()

    @pl.when(jnp.logical_and(t >= LAG, t < n_tiles))
    def _():
        _step(True, True, *args)

    @pl.when(t >= n_tiles)
    def _():
        _step(False, True, *args)


def _const_spec(shape):
    nd = len(shape)
    return pl.BlockSpec(shape, lambda t: (0,) * nd, pipeline_mode=pl.Buffered(1))


def kernel(x, norm1_g, w_in, gmlp_ln_g, gmlp_ln_b, gmlp_w_s, gmlp_b_s, conv_w, conv_b,
           conv_ln_g, conv_ln_b, w_out, norm2_g, w_gate, w_up, w_down, final_norm_g):
    bsz, seq, d = x.shape
    assert d == D_MODEL and seq % TM == 0
    tiles_per_seq = seq // TM
    n_tiles = bsz * tiles_per_seq
    row = lambda a: a.reshape(1, -1)
    bs_tile = jnp.repeat(gmlp_b_s.T, A_HEAD_DIM, axis=1)
    x2 = x.reshape(bsz * seq, d)
    small = (row(norm1_g), row(gmlp_ln_g), row(gmlp_ln_b), gmlp_w_s, bs_tile, conv_w, row(conv_b),
             row(conv_ln_g), row(conv_ln_b), row(norm2_g), row(final_norm_g))
    weights = (w_in, w_out, w_gate, w_up, w_down)
    for w, (_, rows, cols, _) in zip(weights, WEIGHTS):
        assert w.shape == (rows, cols)
    cur_spec = pl.BlockSpec((TM, D_MODEL), lambda t: (jnp.minimum(t, n_tiles - 1), 0))
    prev_spec = pl.BlockSpec((TM, D_MODEL), lambda t: (jnp.maximum(t - LAG, 0), 0))
    in_specs = ([cur_spec, prev_spec] + [_const_spec(a.shape) for a in small]
                + [pl.BlockSpec(memory_space=pl.ANY)] * len(weights))
    out = pl.pallas_call(
        functools.partial(_block_kernel, tiles_per_seq, n_tiles),
        grid=(n_tiles + LAG,),
        in_specs=in_specs,
        out_specs=prev_spec,
        out_shape=jax.ShapeDtypeStruct(x2.shape, x.dtype),
        scratch_shapes=(
            [pltpu.VMEM((rows, cols), jnp.bfloat16) for _, rows, cols, _ in WEIGHTS] + [
                pltpu.VMEM((B_LANE_GROUPS, TM + HALO, LANES), jnp.float32),
                pltpu.VMEM((TM, B_WIDTH), jnp.float32),
                pltpu.VMEM((LAG, TM, D_MODEL), jnp.bfloat16),
                pltpu.VMEM((STAGE_SLOTS - 1, STAGE_ROWS, STAGE_COLS), jnp.float32),
                pltpu.SemaphoreType.DMA((STAGE_SLOTS,)),
            ]),
        compiler_params=pltpu.CompilerParams(
            dimension_semantics=("arbitrary",),
            vmem_limit_bytes=VMEM_LIMIT_BYTES),
        name="hybrid_block",
    )(x2, x2, *small, *weights)
    return out.reshape(x.shape)
```

```python
import functools

import jax
import jax.numpy as jnp
from jax import lax
from jax.experimental import pallas as pl
from jax.experimental.pallas import tpu as pltpu

D_MODEL = 1024
A_WIDTH = 512
B_WIDTH = 512
IN_WIDTH = 2 * A_WIDTH + 2 * B_WIDTH
GMLP_BLOCK = 128
A_HEADS = 4
A_HEAD_DIM = A_WIDTH // A_HEADS
CHUNK = 64
CONV_WIDTH = 31
FFN_HIDDEN = 2816
RMS_EPS = 1e-6
LN_EPS = 1e-5

TM = 512
LAG = 3
HALO = 32
PIECE_ROWS = 32
LANES = 128
B_LANE_GROUPS = B_WIDTH // LANES
MXU_TILE = 256
PACKED_ROWS = 16
FFN_CHUNK = 512
STAGE_ROWS = 512
STAGE_COLS = 1024
STAGE_SLOTS = 4
SUB_COPIES = 4
PIECES_PER_CHUNK_LOAD = 12
VMEM_LIMIT_BYTES = 58 * 1024 * 1024

WEIGHTS = (
    ("w_in", D_MODEL, IN_WIDTH, 0.5),
    ("w_out", D_MODEL, D_MODEL, 1.0),
    ("w_gate", D_MODEL, FFN_HIDDEN, 0.5),
    ("w_up", D_MODEL, FFN_HIDDEN, 1.0),
    ("w_down", FFN_HIDDEN, D_MODEL, 1.0),
)


def _rms(x, g):
    ms = jnp.mean(x * x, axis=-1, keepdims=True)
    return x * lax.rsqrt(ms + RMS_EPS) * g


def _ln(x, g, b):
    mu = jnp.mean(x, axis=-1, keepdims=True)
    xc = x - mu
    var = jnp.mean(xc * xc, axis=-1, keepdims=True)
    return xc * lax.rsqrt(var + LN_EPS) * g + b


def _one_plus_tanh(x):
    return jnp.tanh(x) + 1.0


GELU_C = 0.7978845608028654
GELU_K = 0.044715


def _gelu_from_half(xh):
    inner = (2.0 * GELU_C * xh) * (1.0 + (4.0 * GELU_K) * (xh * xh))
    return xh * _one_plus_tanh(inner)


def _ffn_chunks():
    return [(f0, min(f0 + FFN_CHUNK, FFN_HIDDEN)) for f0 in range(0, FFN_HIDDEN, FFN_CHUNK)]


def _sample(x):
    rows, cols = x.shape
    blocks = [x[r:r + PACKED_ROWS, c:c + LANES]
              for r in range(0, rows, PACKED_ROWS) for c in range(0, cols, LANES)]
    while len(blocks) > 1:
        blocks = [blocks[i] + blocks[i + 1] if i + 1 < len(blocks) else blocks[i]
                  for i in range(0, len(blocks), 2)]
    return blocks[0]


def _tie(never, w, pieces):
    by_row = {}
    for (kt, nt), piece in pieces.items():
        by_row.setdefault(kt, {})[nt] = piece
    segs = []
    r = 0
    for kt in sorted(by_row):
        r0 = kt * MXU_TILE
        if r0 > r:
            segs.append(w[r:r0])
        cols = []
        c = 0
        for nt in sorted(by_row[kt]):
            c0 = nt * MXU_TILE
            if c0 > c:
                cols.append(w[r0:r0 + PACKED_ROWS, c:c0])
            cols.append(jnp.where(never, by_row[kt][nt].astype(w.dtype),
                                  w[r0:r0 + PACKED_ROWS, c0:c0 + LANES]))
            c = c0 + LANES
        if c < w.shape[1]:
            cols.append(w[r0:r0 + PACKED_ROWS, c:])
        segs.append(jnp.concatenate(cols, axis=1))
        r = r0 + PACKED_ROWS
    if r < w.shape[0]:
        segs.append(w[r:])
    return jnp.concatenate(segs, axis=0) if len(segs) > 1 else segs[0]


def _weight_chunks():
    chunks = []
    for wi, (_, rows, cols, _) in enumerate(WEIGHTS):
        for r0 in range(0, rows, STAGE_ROWS):
            for c0 in range(0, cols, STAGE_COLS):
                chunks.append((wi, r0, min(STAGE_ROWS, rows - r0), c0, min(STAGE_COLS, cols - c0)))
    return chunks


class _WeightLoader:
    def __init__(self, w_hbm, w_vmem, slots, sems):
        assert len(slots) == STAGE_SLOTS
        self.w_hbm, self.w_vmem, self.slots, self.sems = w_hbm, w_vmem, slots, sems
        self.chunks = _weight_chunks()
        self.done = 0

    def prime(self):
        for i in range(min(STAGE_SLOTS - 1, len(self.chunks))):
            self._start(i)

    def _copies(self, i):
        wi, r0, nr, c0, nc = self.chunks[i]
        slot = i % STAGE_SLOTS
        sub = nr // SUB_COPIES
        return [pltpu.make_async_copy(
            self.w_hbm[wi].at[pl.ds(r0 + j * sub, sub), pl.ds(c0, nc)],
            self.slots[slot].at[pl.ds(j * sub, sub), pl.ds(0, nc)],
            self.sems.at[slot]) for j in range(SUB_COPIES)]

    def _start(self, i):
        for c in self._copies(i):
            c.start()

    def _wait(self, i):
        for c in self._copies(i):
            c.wait()

    def chunks_of(self, wi):
        return sum(1 for c in self.chunks if c[0] == wi)

    def service(self, n=None):
        n = len(self.chunks) if n is None else n
        for _ in range(n):
            i = self.done
            if i >= len(self.chunks):
                return
            if i + STAGE_SLOTS - 1 < len(self.chunks):
                self._start(i + STAGE_SLOTS - 1)
            self._wait(i)
            wi, r0, nr, c0, nc = self.chunks[i]
            v = self.slots[i % STAGE_SLOTS][0:nr, 0:nc]
            scale = WEIGHTS[wi][3]
            self.w_vmem[wi][r0:r0 + nr, c0:c0 + nc] = (v if scale == 1.0 else scale * v).astype(jnp.bfloat16)
            self.done += 1


def _step(do_mixer, do_channel, t,
          x_ref, xp_ref, n1g_ref, alng_ref, alnb_ref, ws_ref, bs_ref, cw_ref, cb_ref, clng_ref, clnb_ref,
          n2g_ref, fng_ref, w_in_ref, w_out_ref, w_gate_ref, w_up_ref, w_down_ref,
          o_ref, hbuf_ref, cbuf_ref, y_ref, after_piece=None):
    never = t < 0
    f32 = jnp.float32
    bf16 = jnp.bfloat16
    tie_pieces = do_mixer and do_channel
    slot = t % LAG

    if do_channel:
        h1 = xp_ref[...] + jnp.dot(y_ref[slot], w_out_ref[...], preferred_element_type=f32)
        o_ref[...] = h1

    queue = iter(())
    if do_mixer:
        xn = _rms(x_ref[...], n1g_ref[...]).astype(bf16)
        z = jnp.dot(xn, w_in_ref[...], preferred_element_type=f32)

    if do_channel:
        hn2 = _rms(h1, n2g_ref[...]).astype(bf16)

    row_blocks = list(range(0, TM, PIECE_ROWS))
    if do_mixer:
        tap0 = HALO - (CONV_WIDTH - 1)

        def conv_piece(g, r0, after=None):
            if r0 == 0:
                zb = z[:, 2 * A_WIDTH + g * LANES:2 * A_WIDTH + (g + 1) * LANES]
                zg = z[:, 2 * A_WIDTH + B_WIDTH + g * LANES:2 * A_WIDTH + B_WIDTH + (g + 1) * LANES]
                hbuf_ref[g, HALO:HALO + TM, :] = zb * _one_plus_tanh(zg)
            acc = jnp.zeros((PIECE_ROWS, LANES), f32)
            if after is not None:
                acc = jnp.where(never, after, acc)
            for k in range(CONV_WIDTH):
                acc = acc + (hbuf_ref[g, r0 + tap0 + k:r0 + tap0 + k + PIECE_ROWS, :]
                             * cw_ref[k:k + 1, g * LANES:(g + 1) * LANES])
            cbuf_ref[r0:r0 + PIECE_ROWS, g * LANES:(g + 1) * LANES] = acc
            if r0 == row_blocks[-1]:
                hbuf_ref[g, 0:HALO, :] = hbuf_ref[g, TM:TM + HALO, :]
            return _sample(acc) if tie_pieces else None

        half_clng = 0.5 * clng_ref[...]
        half_clnb = 0.5 * clnb_ref[...]

        def conv_norm_piece(r0, after=None):
            cv = cbuf_ref[r0:r0 + PIECE_ROWS, :]
            if after is not None:
                cv = jnp.concatenate([jnp.where(never, after, cv[:, 0:LANES]), cv[:, LANES:]], axis=1)
            hn = _ln(cv + cb_ref[...], half_clng, half_clnb)
            yb = hn * _one_plus_tanh(hn)
            y_ref[slot, r0:r0 + PIECE_ROWS, A_WIDTH:] = yb.astype(bf16)
            return _sample(yb) if tie_pieces else None

        u_blocks, v_blocks = {}, {}

        def gelu_piece(r0, after=None):
            zz = z[r0:r0 + PIECE_ROWS, :2 * A_WIDTH]
            if after is not None:
                zz = jnp.concatenate([jnp.where(never, after, zz[:, 0:LANES]), zz[:, LANES:]], axis=1)
            za = _gelu_from_half(zz)
            u_blocks[r0] = za[:, :A_WIDTH]
            vn = _ln(za[:, A_WIDTH:], alng_ref[...], alnb_ref[...])
            v_blocks[r0] = vn.astype(bf16)
            return _sample(za) if tie_pieces else None

        queue = [functools.partial(conv_piece, g, r0) for g in range(B_LANE_GROUPS) for r0 in row_blocks]
        queue += [functools.partial(conv_norm_piece, r0) for r0 in row_blocks]
        queue += [functools.partial(gelu_piece, r0) for r0 in row_blocks]
        queue = iter(queue)

    def tied(w, tiles, result=None):
        pieces = {}
        for i, tile in enumerate(tiles):
            piece = next(queue, None)
            if piece is not None:
                after = None
                if result is not None:
                    r0 = PIECE_ROWS * (i % len(row_blocks))
                    after = result[r0:r0 + PIECE_ROWS, 0:LANES]
                pieces[tile] = piece(after=after)
        return _tie(never, w, pieces) if pieces else w

    if do_channel:
        k_tiles = D_MODEL // MXU_TILE
        chunk_tiles = [(kt, 0) for kt in range(k_tiles)] + [(kt, 1) for kt in range(1, k_tiles, 2)]
        down_tiles = [(kt, nt) for kt in range(1, FFN_HIDDEN // MXU_TILE) for nt in range(0, D_MODEL // MXU_TILE, 2)]
        acts = []
        result = None
        for f0, f1 in _ffn_chunks():
            tiles = [tl for tl in chunk_tiles if (tl[1] + 1) * MXU_TILE <= f1 - f0]
            gate = jnp.dot(hn2, tied(w_gate_ref[:, f0:f1], tiles, result), preferred_element_type=f32)
            up = jnp.dot(hn2, tied(w_up_ref[:, f0:f1], tiles, result), preferred_element_type=f32)
            acts.append((_one_plus_tanh(gate) * (gate * up)).astype(bf16))
            result = gate
        act = jnp.concatenate(acts, axis=1)
        ffn = jnp.dot(act, tied(w_down_ref[...], down_tiles, result), preferred_element_type=f32)
    for i, piece in enumerate(queue):
        piece()
        if after_piece is not None:
            after_piece(i)

    if do_mixer:
        nblk = TM // GMLP_BLOCK
        u = jnp.concatenate([u_blocks[r0] for r0 in row_blocks], axis=0)
        v = jnp.concatenate([v_blocks[r0] for r0 in row_blocks], axis=0)
        pos_i = lax.broadcasted_iota(jnp.int32, (GMLP_BLOCK, GMLP_BLOCK), 0) // CHUNK
        pos_j = lax.broadcasted_iota(jnp.int32, (GMLP_BLOCK, GMLP_BLOCK), 1) // CHUNK
        mask = pos_j <= pos_i
        for h in range(A_HEADS):
            w = jnp.where(mask, ws_ref[h], 0.0).astype(bf16)
            c0 = h * A_HEAD_DIM
            vh = jnp.concatenate(
                [v[n * GMLP_BLOCK:(n + 1) * GMLP_BLOCK, c0:c0 + A_HEAD_DIM] for n in range(nblk)],
                axis=1)
            sg = jnp.dot(w, vh, preferred_element_type=f32)
            bias = bs_ref[:, c0:c0 + A_HEAD_DIM]
            for n in range(nblk):
                r0 = n * GMLP_BLOCK
                ya = u[r0:r0 + GMLP_BLOCK, c0:c0 + A_HEAD_DIM] * (
                    sg[:, n * A_HEAD_DIM:(n + 1) * A_HEAD_DIM] + bias)
                y_ref[slot, r0:r0 + GMLP_BLOCK, c0:c0 + A_HEAD_DIM] = ya.astype(bf16)

    if do_channel:
        o_ref[...] = _rms(o_ref[...] + ffn, fng_ref[...])


def _block_kernel(tiles_per_seq, n_tiles, *refs):
    n_small = 11
    x_ref, xp_ref = refs[0:2]
    small = refs[2:2 + n_small]
    w_hbm = refs[2 + n_small:2 + n_small + len(WEIGHTS)]
    o_ref = refs[2 + n_small + len(WEIGHTS)]
    w_vmem = refs[3 + n_small + len(WEIGHTS):3 + n_small + 2 * len(WEIGHTS)]
    hbuf_ref, cbuf_ref, y_ref, stage_ref, sems = refs[3 + n_small + 2 * len(WEIGHTS):]
    slots = [stage_ref.at[i] for i in range(STAGE_SLOTS - 1)] + [o_ref]
    t = pl.program_id(0)
    args = (t, x_ref, xp_ref) + tuple(small) + tuple(w_vmem) + (o_ref, hbuf_ref, cbuf_ref, y_ref)

    @pl.when(t % tiles_per_seq == 0)
    def _():
        hbuf_ref[:, 0:HALO, :] = jnp.zeros((B_LANE_GROUPS, HALO, LANES), jnp.float32)

    loader = _WeightLoader(w_hbm, w_vmem, slots, sems)
    for step in range(LAG):
        @pl.when(t == step)
        def _(step=step):
            if step == 0:
                loader.prime()
                loader.service(loader.chunks_of(0))
            _step(True, False, *args,
                  after_piece=lambda i: loader.service(1) if i % PIECES_PER_CHUNK_LOAD == 0 else None)
            if step == LAG - 1:
                loader.service()

    @pl.when(jnp.logical_and(t >= LAG, t < n_tiles))
    def _():
        _step(True, True, *args)

    @pl.when(t >= n_tiles)
    def _():
        _step(False, True, *args)


def _const_spec(shape):
    nd = len(shape)
    return pl.BlockSpec(shape, lambda t: (0,) * nd, pipeline_mode=pl.Buffered(1))


def kernel(x, norm1_g, w_in, gmlp_ln_g, gmlp_ln_b, gmlp_w_s, gmlp_b_s, conv_w, conv_b,
           conv_ln_g, conv_ln_b, w_out, norm2_g, w_gate, w_up, w_down, final_norm_g):
    bsz, seq, d = x.shape
    assert d == D_MODEL and seq % TM == 0
    tiles_per_seq = seq // TM
    n_tiles = bsz * tiles_per_seq
    row = lambda a: a.reshape(1, -1)
    bs_tile = jnp.repeat(gmlp_b_s.T, A_HEAD_DIM, axis=1)
    x2 = x.reshape(bsz * seq, d)
    small = (row(norm1_g), row(gmlp_ln_g), row(gmlp_ln_b), gmlp_w_s, bs_tile, conv_w, row(conv_b),
             row(conv_ln_g), row(conv_ln_b), row(norm2_g), row(final_norm_g))
    weights = (w_in, w_out, w_gate, w_up, w_down)
    for w, (_, rows, cols, _) in zip(weights, WEIGHTS):
        assert w.shape == (rows, cols)
    cur_spec = pl.BlockSpec((TM, D_MODEL), lambda t: (jnp.minimum(t, n_tiles - 1), 0))
    prev_spec = pl.BlockSpec((TM, D_MODEL), lambda t: (jnp.maximum(t - LAG, 0), 0))
    in_specs = ([cur_spec, prev_spec] + [_const_spec(a.shape) for a in small]
                + [pl.BlockSpec(memory_space=pl.ANY)] * len(weights))
    out = pl.pallas_call(
        functools.partial(_block_kernel, tiles_per_seq, n_tiles),
        grid=(n_tiles + LAG,),
        in_specs=in_specs,
        out_specs=prev_spec,
        out_shape=jax.ShapeDtypeStruct(x2.shape, x.dtype),
        scratch_shapes=(
            [pltpu.VMEM((rows, cols), jnp.bfloat16) for _, rows, cols, _ in WEIGHTS] + [
                pltpu.VMEM((B_LANE_GROUPS, TM + HALO, LANES), jnp.float32),
                pltpu.VMEM((TM, B_WIDTH), jnp.float32),
                pltpu.VMEM((LAG, TM, D_MODEL), jnp.bfloat16),
                pltpu.VMEM((STAGE_SLOTS - 1, STAGE_ROWS, STAGE_COLS), jnp.float32),
                pltpu.SemaphoreType.DMA((STAGE_SLOTS,)),
            ]),
        compiler_params=pltpu.CompilerParams(
            dimension_semantics=("arbitrary",),
            vmem_limit_bytes=VMEM_LIMIT_BYTES),
        name="hybrid_block",
    )(x2, x2, *small, *weights)
    return out.reshape(x.shape)
```

```python
import functools

import jax
import jax.numpy as jnp
from jax import lax
from jax.experimental import pallas as pl
from jax.experimental.pallas import tpu as pltpu

D_MODEL = 1024
A_WIDTH = 512
B_WIDTH = 512
IN_WIDTH = 2 * A_WIDTH + 2 * B_WIDTH
GMLP_BLOCK = 128
A_HEADS = 4
A_HEAD_DIM = A_WIDTH // A_HEADS
CHUNK = 64
CONV_WIDTH = 31
FFN_HIDDEN = 2816
RMS_EPS = 1e-6
LN_EPS = 1e-5

TM = 512
LAG = 2
HALO = 32
PIECE_ROWS = 32
LANES = 128
B_LANE_GROUPS = B_WIDTH // LANES
MXU_TILE = 256
PACKED_ROWS = 16
FFN_CHUNK = 512
STAGE_ROWS = 512
STAGE_COLS = 1024
STAGE_SLOTS = 4
SUB_COPIES = 4
PIECES_PER_CHUNK_LOAD = 8
VMEM_LIMIT_BYTES = 58 * 1024 * 1024

WEIGHTS = (
    ("w_in", D_MODEL, IN_WIDTH, 0.5),
    ("w_out", D_MODEL, D_MODEL, 1.0),
    ("w_gate", D_MODEL, FFN_HIDDEN, 0.5),
    ("w_up", D_MODEL, FFN_HIDDEN, 1.0),
    ("w_down", FFN_HIDDEN, D_MODEL, 1.0),
)


def _rms(x, g):
    ms = jnp.mean(x * x, axis=-1, keepdims=True)
    return x * lax.rsqrt(ms + RMS_EPS) * g


def _ln(x, g, b):
    mu = jnp.mean(x, axis=-1, keepdims=True)
    xc = x - mu
    var = jnp.mean(xc * xc, axis=-1, keepdims=True)
    return xc * lax.rsqrt(var + LN_EPS) * g + b


def _one_plus_tanh(x):
    return jnp.tanh(x) + 1.0


GELU_C = 0.7978845608028654
GELU_K = 0.044715


def _gelu_from_half(xh):
    inner = (2.0 * GELU_C * xh) * (1.0 + (4.0 * GELU_K) * (xh * xh))
    return xh * _one_plus_tanh(inner)


def _ffn_chunks():
    return [(f0, min(f0 + FFN_CHUNK, FFN_HIDDEN)) for f0 in range(0, FFN_HIDDEN, FFN_CHUNK)]


def _sample(x):
    rows, cols = x.shape
    blocks = [x[r:r + PACKED_ROWS, c:c + LANES]
              for r in range(0, rows, PACKED_ROWS) for c in range(0, cols, LANES)]
    while len(blocks) > 1:
        blocks = [blocks[i] + blocks[i + 1] if i + 1 < len(blocks) else blocks[i]
                  for i in range(0, len(blocks), 2)]
    return blocks[0]


def _tie(never, w, pieces):
    by_row = {}
    for (kt, nt), piece in pieces.items():
        by_row.setdefault(kt, {})[nt] = piece
    segs = []
    r = 0
    for kt in sorted(by_row):
        r0 = kt * MXU_TILE
        if r0 > r:
            segs.append(w[r:r0])
        cols = []
        c = 0
        for nt in sorted(by_row[kt]):
            c0 = nt * MXU_TILE
            if c0 > c:
                cols.append(w[r0:r0 + PACKED_ROWS, c:c0])
            cols.append(jnp.where(never, by_row[kt][nt].astype(w.dtype),
                                  w[r0:r0 + PACKED_ROWS, c0:c0 + LANES]))
            c = c0 + LANES
        if c < w.shape[1]:
            cols.append(w[r0:r0 + PACKED_ROWS, c:])
        segs.append(jnp.concatenate(cols, axis=1))
        r = r0 + PACKED_ROWS
    if r < w.shape[0]:
        segs.append(w[r:])
    return jnp.concatenate(segs, axis=0) if len(segs) > 1 else segs[0]


def _weight_chunks():
    chunks = []
    for wi, (_, rows, cols, _) in enumerate(WEIGHTS):
        for r0 in range(0, rows, STAGE_ROWS):
            for c0 in range(0, cols, STAGE_COLS):
                chunks.append((wi, r0, min(STAGE_ROWS, rows - r0), c0, min(STAGE_COLS, cols - c0)))
    return chunks


class _WeightLoader:
    def __init__(self, w_hbm, w_vmem, slots, sems):
        assert len(slots) == STAGE_SLOTS
        self.w_hbm, self.w_vmem, self.slots, self.sems = w_hbm, w_vmem, slots, sems
        self.chunks = _weight_chunks()
        self.done = 0

    def prime(self):
        for i in range(min(STAGE_SLOTS - 1, len(self.chunks))):
            self._start(i)

    def _copies(self, i):
        wi, r0, nr, c0, nc = self.chunks[i]
        slot = i % STAGE_SLOTS
        sub = nr // SUB_COPIES
        return [pltpu.make_async_copy(
            self.w_hbm[wi].at[pl.ds(r0 + j * sub, sub), pl.ds(c0, nc)],
            self.slots[slot].at[pl.ds(j * sub, sub), pl.ds(0, nc)],
            self.sems.at[slot]) for j in range(SUB_COPIES)]

    def _start(self, i):
        for c in self._copies(i):
            c.start()

    def _wait(self, i):
        for c in self._copies(i):
            c.wait()

    def chunks_of(self, wi):
        return sum(1 for c in self.chunks if c[0] == wi)

    def service(self, n=None):
        n = len(self.chunks) if n is None else n
        for _ in range(n):
            i = self.done
            if i >= len(self.chunks):
                return
            if i + STAGE_SLOTS - 1 < len(self.chunks):
                self._start(i + STAGE_SLOTS - 1)
            self._wait(i)
            wi, r0, nr, c0, nc = self.chunks[i]
            v = self.slots[i % STAGE_SLOTS][0:nr, 0:nc]
            scale = WEIGHTS[wi][3]
            self.w_vmem[wi][r0:r0 + nr, c0:c0 + nc] = (v if scale == 1.0 else scale * v).astype(jnp.bfloat16)
            self.done += 1


def _step(do_mixer, do_channel, t,
          x_ref, xp_ref, n1g_ref, alng_ref, alnb_ref, ws_ref, bs_ref, cw_ref, cb_ref, clng_ref, clnb_ref,
          n2g_ref, fng_ref, w_in_ref, w_out_ref, w_gate_ref, w_up_ref, w_down_ref,
          o_ref, hbuf_ref, cbuf_ref, y_ref, after_piece=None):
    never = t < 0
    f32 = jnp.float32
    bf16 = jnp.bfloat16
    tie_pieces = do_mixer and do_channel
    slot = t % LAG

    if do_channel:
        h1 = xp_ref[...] + jnp.dot(y_ref[slot], w_out_ref[...], preferred_element_type=f32)
        o_ref[...] = h1

    queue = iter(())
    if do_mixer:
        xn = _rms(x_ref[...], n1g_ref[...]).astype(bf16)
        z = jnp.dot(xn, w_in_ref[...], preferred_element_type=f32)

    if do_channel:
        hn2 = _rms(h1, n2g_ref[...]).astype(bf16)

    row_blocks = list(range(0, TM, PIECE_ROWS))
    if do_mixer:
        tap0 = HALO - (CONV_WIDTH - 1)

        def conv_piece(g, r0, after=None):
            if r0 == 0:
                zb = z[:, 2 * A_WIDTH + g * LANES:2 * A_WIDTH + (g + 1) * LANES]
                zg = z[:, 2 * A_WIDTH + B_WIDTH + g * LANES:2 * A_WIDTH + B_WIDTH + (g + 1) * LANES]
                hbuf_ref[g, HALO:HALO + TM, :] = zb * _one_plus_tanh(zg)
            acc = jnp.zeros((PIECE_ROWS, LANES), f32)
            if after is not None:
                acc = jnp.where(never, after, acc)
            for k in range(CONV_WIDTH):
                acc = acc + (hbuf_ref[g, r0 + tap0 + k:r0 + tap0 + k + PIECE_ROWS, :]
                             * cw_ref[k:k + 1, g * LANES:(g + 1) * LANES])
            cbuf_ref[r0:r0 + PIECE_ROWS, g * LANES:(g + 1) * LANES] = acc
            if r0 == row_blocks[-1]:
                hbuf_ref[g, 0:HALO, :] = hbuf_ref[g, TM:TM + HALO, :]
            return _sample(acc) if tie_pieces else None

        half_clng = 0.5 * clng_ref[...]
        half_clnb = 0.5 * clnb_ref[...]

        def conv_norm_piece(r0, after=None):
            cv = cbuf_ref[r0:r0 + PIECE_ROWS, :]
            if after is not None:
                cv = jnp.concatenate([jnp.where(never, after, cv[:, 0:LANES]), cv[:, LANES:]], axis=1)
            hn = _ln(cv + cb_ref[...], half_clng, half_clnb)
            yb = hn * _one_plus_tanh(hn)
            y_ref[slot, r0:r0 + PIECE_ROWS, A_WIDTH:] = yb.astype(bf16)
            return _sample(yb) if tie_pieces else None

        u_blocks, v_blocks = {}, {}

        def gelu_piece(r0, after=None):
            zz = z[r0:r0 + PIECE_ROWS, :2 * A_WIDTH]
            if after is not None:
                zz = jnp.concatenate([jnp.where(never, after, zz[:, 0:LANES]), zz[:, LANES:]], axis=1)
            za = _gelu_from_half(zz)
            u_blocks[r0] = za[:, :A_WIDTH]
            vn = _ln(za[:, A_WIDTH:], alng_ref[...], alnb_ref[...])
            v_blocks[r0] = vn.astype(bf16)
            return _sample(za) if tie_pieces else None

        queue = [functools.partial(conv_piece, g, r0) for g in range(B_LANE_GROUPS) for r0 in row_blocks]
        queue += [functools.partial(conv_norm_piece, r0) for r0 in row_blocks]
        queue += [functools.partial(gelu_piece, r0) for r0 in row_blocks]
        queue = iter(queue)

    def tied(w, tiles, result=None):
        pieces = {}
        for i, tile in enumerate(tiles):
            piece = next(queue, None)
            if piece is not None:
                after = None
                if result is not None:
                    r0 = PIECE_ROWS * (i % len(row_blocks))
                    after = result[r0:r0 + PIECE_ROWS, 0:LANES]
                pieces[tile] = piece(after=after)
        return _tie(never, w, pieces) if pieces else w

    if do_channel:
        k_tiles = D_MODEL // MXU_TILE
        chunk_tiles = [(kt, 0) for kt in range(k_tiles)] + [(kt, 1) for kt in range(1, k_tiles, 2)]
        down_tiles = [(kt, nt) for kt in range(1, FFN_HIDDEN // MXU_TILE) for nt in range(0, D_MODEL // MXU_TILE, 2)]
        acts = []
        result = None
        for f0, f1 in _ffn_chunks():
            tiles = [tl for tl in chunk_tiles if (tl[1] + 1) * MXU_TILE <= f1 - f0]
            gate = jnp.dot(hn2, tied(w_gate_ref[:, f0:f1], tiles, result), preferred_element_type=f32)
            up = jnp.dot(hn2, tied(w_up_ref[:, f0:f1], tiles, result), preferred_element_type=f32)
            gb = gate.astype(bf16)
            acts.append(_one_plus_tanh(gb) * (gb * up.astype(bf16)))
            result = gate
        act = jnp.concatenate(acts, axis=1)
        ffn = jnp.dot(act, tied(w_down_ref[...], down_tiles, result), preferred_element_type=f32)
    for i, piece in enumerate(queue):
        piece()
        if after_piece is not None:
            after_piece(i)

    if do_mixer:
        nblk = TM // GMLP_BLOCK
        u = jnp.concatenate([u_blocks[r0] for r0 in row_blocks], axis=0)
        v = jnp.concatenate([v_blocks[r0] for r0 in row_blocks], axis=0)
        pos_i = lax.broadcasted_iota(jnp.int32, (GMLP_BLOCK, GMLP_BLOCK), 0) // CHUNK
        pos_j = lax.broadcasted_iota(jnp.int32, (GMLP_BLOCK, GMLP_BLOCK), 1) // CHUNK
        mask = pos_j <= pos_i
        for h in range(A_HEADS):
            w = jnp.where(mask, ws_ref[h], 0.0).astype(bf16)
            c0 = h * A_HEAD_DIM
            vh = jnp.concatenate(
                [v[n * GMLP_BLOCK:(n + 1) * GMLP_BLOCK, c0:c0 + A_HEAD_DIM] for n in range(nblk)],
                axis=1)
            sg = jnp.dot(w, vh, preferred_element_type=f32)
            bias = bs_ref[:, c0:c0 + A_HEAD_DIM]
            for n in range(nblk):
                r0 = n * GMLP_BLOCK
                ya = u[r0:r0 + GMLP_BLOCK, c0:c0 + A_HEAD_DIM] * (
                    sg[:, n * A_HEAD_DIM:(n + 1) * A_HEAD_DIM] + bias)
                y_ref[slot, r0:r0 + GMLP_BLOCK, c0:c0 + A_HEAD_DIM] = ya.astype(bf16)

    if do_channel:
        o_ref[...] = _rms(o_ref[...] + ffn, fng_ref[...])


def _block_kernel(tiles_per_seq, n_tiles, *refs):
    n_small = 11
    x_ref, xp_ref = refs[0:2]
    small = refs[2:2 + n_small]
    w_hbm = refs[2 + n_small:2 + n_small + len(WEIGHTS)]
    o_ref = refs[2 + n_small + len(WEIGHTS)]
    w_vmem = refs[3 + n_small + len(WEIGHTS):3 + n_small + 2 * len(WEIGHTS)]
    hbuf_ref, cbuf_ref, y_ref, stage_ref, sems = refs[3 + n_small + 2 * len(WEIGHTS):]
    slots = [stage_ref.at[i] for i in range(STAGE_SLOTS - 1)] + [o_ref]
    t = pl.program_id(0)
    args = (t, x_ref, xp_ref) + tuple(small) + tuple(w_vmem) + (o_ref, hbuf_ref, cbuf_ref, y_ref)

    @pl.when(t % tiles_per_seq == 0)
    def _():
        hbuf_ref[:, 0:HALO, :] = jnp.zeros((B_LANE_GROUPS, HALO, LANES), jnp.float32)

    loader = _WeightLoader(w_hbm, w_vmem, slots, sems)
    for step in range(LAG):
        @pl.when(t == step)
        def _(step=step):
            if step == 0:
                loader.prime()
                loader.service(loader.chunks_of(0))
            _step(True, False, *args,
                  after_piece=lambda i: loader.service(1) if i % PIECES_PER_CHUNK_LOAD == 0 else None)
            if step == LAG - 1:
                loader.service()

    @pl.when(jnp.logical_and(t >= LAG, t < n_tiles))
    def _():
        _step(True, True, *args)

    @pl.when(t >= n_tiles)
    def _():
        _step(False, True, *args)


def _const_spec(shape):
    nd = len(shape)
    return pl.BlockSpec(shape, lambda t: (0,) * nd, pipeline_mode=pl.Buffered(1))


def kernel(x, norm1_g, w_in, gmlp_ln_g, gmlp_ln_b, gmlp_w_s, gmlp_b_s, conv_w, conv_b,
           conv_ln_g, conv_ln_b, w_out, norm2_g, w_gate, w_up, w_down, final_norm_g):
    bsz, seq, d = x.shape
    assert d == D_MODEL and seq % TM == 0
    tiles_per_seq = seq // TM
    n_tiles = bsz * tiles_per_seq
    row = lambda a: a.reshape(1, -1)
    bs_tile = jnp.repeat(gmlp_b_s.T, A_HEAD_DIM, axis=1)
    x2 = x.reshape(bsz * seq, d)
    small = (row(norm1_g), row(gmlp_ln_g), row(gmlp_ln_b), gmlp_w_s, bs_tile, conv_w, row(conv_b),
             row(conv_ln_g), row(conv_ln_b), row(norm2_g), row(final_norm_g))
    weights = (w_in, w_out, w_gate, w_up, w_down)
    for w, (_, rows, cols, _) in zip(weights, WEIGHTS):
        assert w.shape == (rows, cols)
    cur_spec = pl.BlockSpec((TM, D_MODEL), lambda t: (jnp.minimum(t, n_tiles - 1), 0))
    prev_spec = pl.BlockSpec((TM, D_MODEL), lambda t: (jnp.maximum(t - LAG, 0), 0))
    in_specs = ([cur_spec, prev_spec] + [_const_spec(a.shape) for a in small]
                + [pl.BlockSpec(memory_space=pl.ANY)] * len(weights))
    out = pl.pallas_call(
        functools.partial(_block_kernel, tiles_per_seq, n_tiles),
        grid=(n_tiles + LAG,),
        in_specs=in_specs,
        out_specs=prev_spec,
        out_shape=jax.ShapeDtypeStruct(x2.shape, x.dtype),
        scratch_shapes=(
            [pltpu.VMEM((rows, cols), jnp.bfloat16) for _, rows, cols, _ in WEIGHTS] + [
                pltpu.VMEM((B_LANE_GROUPS, TM + HALO, LANES), jnp.float32),
                pltpu.VMEM((TM, B_WIDTH), jnp.float32),
                pltpu.VMEM((LAG, TM, D_MODEL), jnp.bfloat16),
                pltpu.VMEM((STAGE_SLOTS - 1, STAGE_ROWS, STAGE_COLS), jnp.float32),
                pltpu.SemaphoreType.DMA((STAGE_SLOTS,)),
            ]),
        compiler_params=pltpu.CompilerParams(
            dimension_semantics=("arbitrary",),
            vmem_limit_bytes=VMEM_LIMIT_BYTES),
        name="hybrid_block",
    )(x2, x2, *small, *weights)
    return out.reshape(x.shape)
```

```python
import functools

import jax
import jax.numpy as jnp
from jax import lax
from jax.experimental import pallas as pl
from jax.experimental.pallas import tpu as pltpu

D_MODEL = 1024
A_WIDTH = 512
B_WIDTH = 512
IN_WIDTH = 2 * A_WIDTH + 2 * B_WIDTH
GMLP_BLOCK = 128
A_HEADS = 4
A_HEAD_DIM = A_WIDTH // A_HEADS
CHUNK = 64
CONV_WIDTH = 31
FFN_HIDDEN = 2816
RMS_EPS = 1e-6
LN_EPS = 1e-5

TM = 512
LAG = 2
HALO = 32
PIECE_ROWS = 32
LANES = 128
B_LANE_GROUPS = B_WIDTH // LANES
MXU_TILE = 256
PACKED_ROWS = 16
FFN_CHUNK = 512
STAGE_ROWS = 512
STAGE_COLS = 1024
STAGE_SLOTS = 4
SUB_COPIES = 4
PIECES_PER_CHUNK_LOAD = 8
VMEM_LIMIT_BYTES = 58 * 1024 * 1024

WEIGHTS = (
    ("w_in", D_MODEL, IN_WIDTH, 0.5),
    ("w_out", D_MODEL, D_MODEL, 1.0),
    ("w_gate", D_MODEL, FFN_HIDDEN, 0.5),
    ("w_up", D_MODEL, FFN_HIDDEN, 1.0),
    ("w_down", FFN_HIDDEN, D_MODEL, 1.0),
)


def _rms(x, g):
    ms = jnp.mean(x * x, axis=-1, keepdims=True)
    return x * lax.rsqrt(ms + RMS_EPS) * g


def _ln(x, g, b):
    mu = jnp.mean(x, axis=-1, keepdims=True)
    xc = x - mu
    var = jnp.mean(xc * xc, axis=-1, keepdims=True)
    return xc * lax.rsqrt(var + LN_EPS) * g + b


def _one_plus_tanh(x):
    return jnp.tanh(x) + 1.0


GELU_C = 0.7978845608028654
GELU_K = 0.044715


def _gelu_from_half(xh):
    inner = (2.0 * GELU_C * xh) * (1.0 + (4.0 * GELU_K) * (xh * xh))
    return xh * _one_plus_tanh(inner)


def _ffn_chunks():
    return [(f0, min(f0 + FFN_CHUNK, FFN_HIDDEN)) for f0 in range(0, FFN_HIDDEN, FFN_CHUNK)]


def _sample(x):
    rows, cols = x.shape
    blocks = [x[r:r + PACKED_ROWS, c:c + LANES]
              for r in range(0, rows, PACKED_ROWS) for c in range(0, cols, LANES)]
    while len(blocks) > 1:
        blocks = [blocks[i] + blocks[i + 1] if i + 1 < len(blocks) else blocks[i]
                  for i in range(0, len(blocks), 2)]
    return blocks[0]


def _tie(never, w, pieces):
    by_row = {}
    for (kt, nt), piece in pieces.items():
        by_row.setdefault(kt, {})[nt] = piece
    segs = []
    r = 0
    for kt in sorted(by_row):
        r0 = kt * MXU_TILE
        if r0 > r:
            segs.append(w[r:r0])
        cols = []
        c = 0
        for nt in sorted(by_row[kt]):
            c0 = nt * MXU_TILE
            if c0 > c:
                cols.append(w[r0:r0 + PACKED_ROWS, c:c0])
            cols.append(jnp.where(never, by_row[kt][nt].astype(w.dtype),
                                  w[r0:r0 + PACKED_ROWS, c0:c0 + LANES]))
            c = c0 + LANES
        if c < w.shape[1]:
            cols.append(w[r0:r0 + PACKED_ROWS, c:])
        segs.append(jnp.concatenate(cols, axis=1))
        r = r0 + PACKED_ROWS
    if r < w.shape[0]:
        segs.append(w[r:])
    return jnp.concatenate(segs, axis=0) if len(segs) > 1 else segs[0]


def _weight_chunks():
    chunks = []
    for wi, (_, rows, cols, _) in enumerate(WEIGHTS):
        for r0 in range(0, rows, STAGE_ROWS):
            for c0 in range(0, cols, STAGE_COLS):
                chunks.append((wi, r0, min(STAGE_ROWS, rows - r0), c0, min(STAGE_COLS, cols - c0)))
    return chunks


class _WeightLoader:
    def __init__(self, w_hbm, w_vmem, slots, sems):
        assert len(slots) == STAGE_SLOTS
        self.w_hbm, self.w_vmem, self.slots, self.sems = w_hbm, w_vmem, slots, sems
        self.chunks = _weight_chunks()
        self.done = 0

    def prime(self):
        for i in range(min(STAGE_SLOTS - 1, len(self.chunks))):
            self._start(i)

    def _copies(self, i):
        wi, r0, nr, c0, nc = self.chunks[i]
        slot = i % STAGE_SLOTS
        sub = nr // SUB_COPIES
        return [pltpu.make_async_copy(
            self.w_hbm[wi].at[pl.ds(r0 + j * sub, sub), pl.ds(c0, nc)],
            self.slots[slot].at[pl.ds(j * sub, sub), pl.ds(0, nc)],
            self.sems.at[slot]) for j in range(SUB_COPIES)]

    def _start(self, i):
        for j, c in enumerate(self._copies(i)):
            c.start(priority=j % 2)

    def _wait(self, i):
        for c in self._copies(i):
            c.wait()

    def chunks_of(self, wi):
        return sum(1 for c in self.chunks if c[0] == wi)

    def service(self, n=None):
        n = len(self.chunks) if n is None else n
        for _ in range(n):
            i = self.done
            if i >= len(self.chunks):
                return
            if i + STAGE_SLOTS - 1 < len(self.chunks):
                self._start(i + STAGE_SLOTS - 1)
            self._wait(i)
            wi, r0, nr, c0, nc = self.chunks[i]
            v = self.slots[i % STAGE_SLOTS][0:nr, 0:nc]
            scale = WEIGHTS[wi][3]
            self.w_vmem[wi][r0:r0 + nr, c0:c0 + nc] = (v if scale == 1.0 else scale * v).astype(jnp.bfloat16)
            self.done += 1


def _step(do_mixer, do_channel, t,
          x_ref, xp_ref, n1g_ref, alng_ref, alnb_ref, ws_ref, bs_ref, cw_ref, cb_ref, clng_ref, clnb_ref,
          n2g_ref, fng_ref, w_in_ref, w_out_ref, w_gate_ref, w_up_ref, w_down_ref,
          o_ref, hbuf_ref, cbuf_ref, y_ref, after_piece=None):
    never = t < 0
    f32 = jnp.float32
    bf16 = jnp.bfloat16
    tie_pieces = do_mixer and do_channel
    slot = t % LAG

    if do_channel:
        h1 = xp_ref[...] + jnp.dot(y_ref[slot], w_out_ref[...], preferred_element_type=f32)
        o_ref[...] = h1

    queue = iter(())
    if do_mixer:
        xn = _rms(x_ref[...], n1g_ref[...]).astype(bf16)
        z = jnp.dot(xn, w_in_ref[...], preferred_element_type=f32)

    if do_channel:
        hn2 = _rms(h1, n2g_ref[...]).astype(bf16)

    row_blocks = list(range(0, TM, PIECE_ROWS))
    if do_mixer:
        tap0 = HALO - (CONV_WIDTH - 1)

        def conv_piece(g, r0, after=None):
            if r0 == 0:
                zb = z[:, 2 * A_WIDTH + g * LANES:2 * A_WIDTH + (g + 1) * LANES]
                zg = z[:, 2 * A_WIDTH + B_WIDTH + g * LANES:2 * A_WIDTH + B_WIDTH + (g + 1) * LANES]
                hbuf_ref[g, HALO:HALO + TM, :] = zb * _one_plus_tanh(zg)
            acc = jnp.zeros((PIECE_ROWS, LANES), f32)
            if after is not None:
                acc = jnp.where(never, after, acc)
            for k in range(CONV_WIDTH):
                acc = acc + (hbuf_ref[g, r0 + tap0 + k:r0 + tap0 + k + PIECE_ROWS, :]
                             * cw_ref[k:k + 1, g * LANES:(g + 1) * LANES])
            cbuf_ref[r0:r0 + PIECE_ROWS, g * LANES:(g + 1) * LANES] = acc
            if r0 == row_blocks[-1]:
                hbuf_ref[g, 0:HALO, :] = hbuf_ref[g, TM:TM + HALO, :]
            return _sample(acc) if tie_pieces else None

        half_clng = 0.5 * clng_ref[...]
        half_clnb = 0.5 * clnb_ref[...]

        def conv_norm_piece(r0, after=None):
            cv = cbuf_ref[r0:r0 + PIECE_ROWS, :]
            if after is not None:
                cv = jnp.concatenate([jnp.where(never, after, cv[:, 0:LANES]), cv[:, LANES:]], axis=1)
            hn = _ln(cv + cb_ref[...], half_clng, half_clnb)
            yb = hn * _one_plus_tanh(hn)
            y_ref[slot, r0:r0 + PIECE_ROWS, A_WIDTH:] = yb.astype(bf16)
            return _sample(yb) if tie_pieces else None

        u_blocks, v_blocks = {}, {}

        def gelu_piece(r0, after=None):
            zz = z[r0:r0 + PIECE_ROWS, :2 * A_WIDTH]
            if after is not None:
                zz = jnp.concatenate([jnp.where(never, after, zz[:, 0:LANES]), zz[:, LANES:]], axis=1)
            za = _gelu_from_half(zz)
            u_blocks[r0] = za[:, :A_WIDTH]
            vn = _ln(za[:, A_WIDTH:], alng_ref[...], alnb_ref[...])
            v_blocks[r0] = vn.astype(bf16)
            return _sample(za) if tie_pieces else None

        queue = [functools.partial(conv_piece, g, r0) for g in range(B_LANE_GROUPS) for r0 in row_blocks]
        queue += [functools.partial(conv_norm_piece, r0) for r0 in row_blocks]
        queue += [functools.partial(gelu_piece, r0) for r0 in row_blocks]
        queue = iter(queue)

    def tied(w, tiles, result=None):
        pieces = {}
        for i, tile in enumerate(tiles):
            piece = next(queue, None)
            if piece is not None:
                after = None
                if result is not None:
                    r0 = PIECE_ROWS * (i % len(row_blocks))
                    after = result[r0:r0 + PIECE_ROWS, 0:LANES]
                pieces[tile] = piece(after=after)
        return _tie(never, w, pieces) if pieces else w

    if do_channel:
        k_tiles = D_MODEL // MXU_TILE
        chunk_tiles = [(kt, 0) for kt in range(k_tiles)] + [(kt, 1) for kt in range(1, k_tiles, 2)]
        down_tiles = [(kt, nt) for kt in range(1, FFN_HIDDEN // MXU_TILE) for nt in range(0, D_MODEL // MXU_TILE, 2)]
        acts = []
        result = None
        for f0, f1 in _ffn_chunks():
            tiles = [tl for tl in chunk_tiles if (tl[1] + 1) * MXU_TILE <= f1 - f0]
            gate = jnp.dot(hn2, tied(w_gate_ref[:, f0:f1], tiles, result), preferred_element_type=f32)
            up = jnp.dot(hn2, tied(w_up_ref[:, f0:f1], tiles, result), preferred_element_type=f32)
            gb = gate.astype(bf16)
            acts.append(_one_plus_tanh(gb) * (gb * up.astype(bf16)))
            result = gate
        act = jnp.concatenate(acts, axis=1)
        ffn = jnp.dot(act, tied(w_down_ref[...], down_tiles, result), preferred_element_type=f32)
    for i, piece in enumerate(queue):
        piece()
        if after_piece is not None:
            after_piece(i)

    if do_mixer:
        nblk = TM // GMLP_BLOCK
        u = jnp.concatenate([u_blocks[r0] for r0 in row_blocks], axis=0)
        v = jnp.concatenate([v_blocks[r0] for r0 in row_blocks], axis=0)
        pos_i = lax.broadcasted_iota(jnp.int32, (GMLP_BLOCK, GMLP_BLOCK), 0) // CHUNK
        pos_j = lax.broadcasted_iota(jnp.int32, (GMLP_BLOCK, GMLP_BLOCK), 1) // CHUNK
        mask = pos_j <= pos_i
        for h in range(A_HEADS):
            w = jnp.where(mask, ws_ref[h], 0.0).astype(bf16)
            c0 = h * A_HEAD_DIM
            vh = jnp.concatenate(
                [v[n * GMLP_BLOCK:(n + 1) * GMLP_BLOCK, c0:c0 + A_HEAD_DIM] for n in range(nblk)],
                axis=1)
            sg = jnp.dot(w, vh, preferred_element_type=f32)
            bias = bs_ref[:, c0:c0 + A_HEAD_DIM]
            for n in range(nblk):
                r0 = n * GMLP_BLOCK
                ya = u[r0:r0 + GMLP_BLOCK, c0:c0 + A_HEAD_DIM] * (
                    sg[:, n * A_HEAD_DIM:(n + 1) * A_HEAD_DIM] + bias)
                y_ref[slot, r0:r0 + GMLP_BLOCK, c0:c0 + A_HEAD_DIM] = ya.astype(bf16)

    if do_channel:
        o_ref[...] = _rms(o_ref[...] + ffn, fng_ref[...])


def _block_kernel(tiles_per_seq, n_tiles, *refs):
    n_small = 11
    x_ref, xp_ref = refs[0:2]
    small = refs[2:2 + n_small]
    w_hbm = refs[2 + n_small:2 + n_small + len(WEIGHTS)]
    o_ref = refs[2 + n_small + len(WEIGHTS)]
    w_vmem = refs[3 + n_small + len(WEIGHTS):3 + n_small + 2 * len(WEIGHTS)]
    hbuf_ref, cbuf_ref, y_ref, stage_ref, sems = refs[3 + n_small + 2 * len(WEIGHTS):]
    slots = [stage_ref.at[i] for i in range(STAGE_SLOTS - 1)] + [o_ref]
    t = pl.program_id(0)
    args = (t, x_ref, xp_ref) + tuple(small) + tuple(w_vmem) + (o_ref, hbuf_ref, cbuf_ref, y_ref)

    @pl.when(t % tiles_per_seq == 0)
    def _():
        hbuf_ref[:, 0:HALO, :] = jnp.zeros((B_LANE_GROUPS, HALO, LANES), jnp.float32)

    loader = _WeightLoader(w_hbm, w_vmem, slots, sems)
    for step in range(LAG):
        @pl.when(t == step)
        def _(step=step):
            if step == 0:
                loader.prime()
                loader.service(loader.chunks_of(0))
            _step(True, False, *args,
                  after_piece=lambda i: loader.service(1) if i % PIECES_PER_CHUNK_LOAD == 0 else None)
            if step == LAG - 1:
                loader.service()

    @pl.when(jnp.logical_and(t >= LAG, t < n_tiles))
    def _():
        _step(True, True, *args)

    @pl.when(t >= n_tiles)
    def _():
        _step(False, True, *args)


def _const_spec(shape):
    nd = len(shape)
    return pl.BlockSpec(shape, lambda t: (0,) * nd, pipeline_mode=pl.Buffered(1))


def kernel(x, norm1_g, w_in, gmlp_ln_g, gmlp_ln_b, gmlp_w_s, gmlp_b_s, conv_w, conv_b,
           conv_ln_g, conv_ln_b, w_out, norm2_g, w_gate, w_up, w_down, final_norm_g):
    bsz, seq, d = x.shape
    assert d == D_MODEL and seq % TM == 0
    tiles_per_seq = seq // TM
    n_tiles = bsz * tiles_per_seq
    row = lambda a: a.reshape(1, -1)
    bs_tile = jnp.repeat(gmlp_b_s.T, A_HEAD_DIM, axis=1)
    x2 = x.reshape(bsz * seq, d)
    small = (row(norm1_g), row(gmlp_ln_g), row(gmlp_ln_b), gmlp_w_s, bs_tile, conv_w, row(conv_b),
             row(conv_ln_g), row(conv_ln_b), row(norm2_g), row(final_norm_g))
    weights = (w_in, w_out, w_gate, w_up, w_down)
    for w, (_, rows, cols, _) in zip(weights, WEIGHTS):
        assert w.shape == (rows, cols)
    cur_spec = pl.BlockSpec((TM, D_MODEL), lambda t: (jnp.minimum(t, n_tiles - 1), 0))
    prev_spec = pl.BlockSpec((TM, D_MODEL), lambda t: (jnp.maximum(t - LAG, 0), 0))
    in_specs = ([cur_spec, prev_spec] + [_const_spec(a.shape) for a in small]
                + [pl.BlockSpec(memory_space=pl.ANY)] * len(weights))
    out = pl.pallas_call(
        functools.partial(_block_kernel, tiles_per_seq, n_tiles),
        grid=(n_tiles + LAG,),
        in_specs=in_specs,
        out_specs=prev_spec,
        out_shape=jax.ShapeDtypeStruct(x2.shape, x.dtype),
        scratch_shapes=(
            [pltpu.VMEM((rows, cols), jnp.bfloat16) for _, rows, cols, _ in WEIGHTS] + [
                pltpu.VMEM((B_LANE_GROUPS, TM + HALO, LANES), jnp.float32),
                pltpu.VMEM((TM, B_WIDTH), jnp.float32),
                pltpu.VMEM((LAG, TM, D_MODEL), jnp.bfloat16),
                pltpu.VMEM((STAGE_SLOTS - 1, STAGE_ROWS, STAGE_COLS), jnp.float32),
                pltpu.SemaphoreType.DMA((STAGE_SLOTS,)),
            ]),
        compiler_params=pltpu.CompilerParams(
            dimension_semantics=("arbitrary",),
            vmem_limit_bytes=VMEM_LIMIT_BYTES),
        name="hybrid_block",
    )(x2, x2, *small, *weights)
    return out.reshape(x.shape)
```

```python
import functools

import jax
import jax.numpy as jnp
from jax import lax
from jax.experimental import pallas as pl
from jax.experimental.pallas import tpu as pltpu

D_MODEL = 1024
A_WIDTH = 512
B_WIDTH = 512
IN_WIDTH = 2 * A_WIDTH + 2 * B_WIDTH
GMLP_BLOCK = 128
A_HEADS = 4
A_HEAD_DIM = A_WIDTH // A_HEADS
CHUNK = 64
CONV_WIDTH = 31
FFN_HIDDEN = 2816
RMS_EPS = 1e-6
LN_EPS = 1e-5

TM = 512
LAG = 2
HALO = 32
PIECE_ROWS = 32
LANES = 128
B_LANE_GROUPS = B_WIDTH // LANES
MXU_TILE = 256
PACKED_ROWS = 16
FFN_CHUNK = 512
STAGE_ROWS = 512
STAGE_COLS = 1024
STAGE_SLOTS = 4
SUB_COPIES = 4
PIECES_PER_CHUNK_LOAD = 8
VMEM_LIMIT_BYTES = 58 * 1024 * 1024

WEIGHTS = (
    ("w_in", D_MODEL, IN_WIDTH, 0.5),
    ("w_out", D_MODEL, D_MODEL, 1.0),
    ("w_gate", D_MODEL, FFN_HIDDEN, 0.5),
    ("w_up", D_MODEL, FFN_HIDDEN, 1.0),
    ("w_down", FFN_HIDDEN, D_MODEL, 1.0),
)


def _rms(x, g):
    ms = jnp.mean(x * x, axis=-1, keepdims=True)
    return x * lax.rsqrt(ms + RMS_EPS) * g


def _ln(x, g, b):
    mu = jnp.mean(x, axis=-1, keepdims=True)
    xc = x - mu
    var = jnp.mean(xc * xc, axis=-1, keepdims=True)
    return xc * lax.rsqrt(var + LN_EPS) * g + b


def _one_plus_tanh(x):
    return jnp.tanh(x) + 1.0


GELU_C = 0.7978845608028654
GELU_K = 0.044715


def _gelu_from_half(xh):
    inner = (2.0 * GELU_C * xh) * (1.0 + (4.0 * GELU_K) * (xh * xh))
    return xh * _one_plus_tanh(inner)


def _ffn_chunks():
    return [(f0, min(f0 + FFN_CHUNK, FFN_HIDDEN)) for f0 in range(0, FFN_HIDDEN, FFN_CHUNK)]


def _sample(x):
    rows, cols = x.shape
    blocks = [x[r:r + PACKED_ROWS, c:c + LANES]
              for r in range(0, rows, PACKED_ROWS) for c in range(0, cols, LANES)]
    while len(blocks) > 1:
        blocks = [blocks[i] + blocks[i + 1] if i + 1 < len(blocks) else blocks[i]
                  for i in range(0, len(blocks), 2)]
    return blocks[0]


def _tie(never, w, pieces):
    by_row = {}
    for (kt, nt), piece in pieces.items():
        by_row.setdefault(kt, {})[nt] = piece
    segs = []
    r = 0
    for kt in sorted(by_row):
        r0 = kt * MXU_TILE
        if r0 > r:
            segs.append(w[r:r0])
        cols = []
        c = 0
        for nt in sorted(by_row[kt]):
            c0 = nt * MXU_TILE
            if c0 > c:
                cols.append(w[r0:r0 + PACKED_ROWS, c:c0])
            cols.append(jnp.where(never, by_row[kt][nt].astype(w.dtype),
                                  w[r0:r0 + PACKED_ROWS, c0:c0 + LANES]))
            c = c0 + LANES
        if c < w.shape[1]:
            cols.append(w[r0:r0 + PACKED_ROWS, c:])
        segs.append(jnp.concatenate(cols, axis=1))
        r = r0 + PACKED_ROWS
    if r < w.shape[0]:
        segs.append(w[r:])
    return jnp.concatenate(segs, axis=0) if len(segs) > 1 else segs[0]


def _weight_chunks():
    chunks = []
    for wi, (_, rows, cols, _) in enumerate(WEIGHTS):
        for r0 in range(0, rows, STAGE_ROWS):
            for c0 in range(0, cols, STAGE_COLS):
                chunks.append((wi, r0, min(STAGE_ROWS, rows - r0), c0, min(STAGE_COLS, cols - c0)))
    return chunks


class _WeightLoader:
    def __init__(self, w_hbm, w_vmem, slots, sems):
        assert len(slots) == STAGE_SLOTS
        self.w_hbm, self.w_vmem, self.slots, self.sems = w_hbm, w_vmem, slots, sems
        self.chunks = _weight_chunks()
        self.done = 0

    def prime(self):
        for i in range(min(STAGE_SLOTS - 1, len(self.chunks))):
            self._start(i)

    def _copies(self, i):
        wi, r0, nr, c0, nc = self.chunks[i]
        slot = i % STAGE_SLOTS
        sub = nr // SUB_COPIES
        return [pltpu.make_async_copy(
            self.w_hbm[wi].at[pl.ds(r0 + j * sub, sub), pl.ds(c0, nc)],
            self.slots[slot].at[pl.ds(j * sub, sub), pl.ds(0, nc)],
            self.sems.at[slot]) for j in range(SUB_COPIES)]

    def _start(self, i):
        for c in self._copies(i):
            c.start(priority=1)

    def _wait(self, i):
        for c in self._copies(i):
            c.wait()

    def chunks_of(self, wi):
        return sum(1 for c in self.chunks if c[0] == wi)

    def service(self, n=None):
        n = len(self.chunks) if n is None else n
        for _ in range(n):
            i = self.done
            if i >= len(self.chunks):
                return
            if i + STAGE_SLOTS - 1 < len(self.chunks):
                self._start(i + STAGE_SLOTS - 1)
            self._wait(i)
            wi, r0, nr, c0, nc = self.chunks[i]
            v = self.slots[i % STAGE_SLOTS][0:nr, 0:nc]
            scale = WEIGHTS[wi][3]
            self.w_vmem[wi][r0:r0 + nr, c0:c0 + nc] = (v if scale == 1.0 else scale * v).astype(jnp.bfloat16)
            self.done += 1


def _step(do_mixer, do_channel, t,
          x_ref, xp_ref, n1g_ref, alng_ref, alnb_ref, ws_ref, bs_ref, cw_ref, cb_ref, clng_ref, clnb_ref,
          n2g_ref, fng_ref, w_in_ref, w_out_ref, w_gate_ref, w_up_ref, w_down_ref,
          o_ref, hbuf_ref, cbuf_ref, y_ref, after_piece=None):
    never = t < 0
    f32 = jnp.float32
    bf16 = jnp.bfloat16
    tie_pieces = do_mixer and do_channel
    slot = t % LAG

    if do_channel:
        h1 = xp_ref[...] + jnp.dot(y_ref[slot], w_out_ref[...], preferred_element_type=f32)
        o_ref[...] = h1

    queue = iter(())
    if do_mixer:
        xn = _rms(x_ref[...], n1g_ref[...]).astype(bf16)
        z = jnp.dot(xn, w_in_ref[...], preferred_element_type=f32)

    if do_channel:
        hn2 = _rms(h1, n2g_ref[...]).astype(bf16)

    row_blocks = list(range(0, TM, PIECE_ROWS))
    if do_mixer:
        tap0 = HALO - (CONV_WIDTH - 1)

        def conv_piece(g, r0, after=None):
            if r0 == 0:
                zb = z[:, 2 * A_WIDTH + g * LANES:2 * A_WIDTH + (g + 1) * LANES]
                zg = z[:, 2 * A_WIDTH + B_WIDTH + g * LANES:2 * A_WIDTH + B_WIDTH + (g + 1) * LANES]
                hbuf_ref[g, HALO:HALO + TM, :] = zb * _one_plus_tanh(zg)
            acc = jnp.zeros((PIECE_ROWS, LANES), f32)
            if after is not None:
                acc = jnp.where(never, after, acc)
            for k in range(CONV_WIDTH):
                acc = acc + (hbuf_ref[g, r0 + tap0 + k:r0 + tap0 + k + PIECE_ROWS, :]
                             * cw_ref[k:k + 1, g * LANES:(g + 1) * LANES])
            cbuf_ref[r0:r0 + PIECE_ROWS, g * LANES:(g + 1) * LANES] = acc
            if r0 == row_blocks[-1]:
                hbuf_ref[g, 0:HALO, :] = hbuf_ref[g, TM:TM + HALO, :]
            return _sample(acc) if tie_pieces else None

        half_clng = 0.5 * clng_ref[...]
        half_clnb = 0.5 * clnb_ref[...]

        def conv_norm_piece(r0, after=None):
            cv = cbuf_ref[r0:r0 + PIECE_ROWS, :]
            if after is not None:
                cv = jnp.concatenate([jnp.where(never, after, cv[:, 0:LANES]), cv[:, LANES:]], axis=1)
            hn = _ln(cv + cb_ref[...], half_clng, half_clnb)
            yb = hn * _one_plus_tanh(hn)
            y_ref[slot, r0:r0 + PIECE_ROWS, A_WIDTH:] = yb.astype(bf16)
            return _sample(yb) if tie_pieces else None

        u_blocks, v_blocks = {}, {}

        def gelu_piece(r0, after=None):
            zz = z[r0:r0 + PIECE_ROWS, :2 * A_WIDTH]
            if after is not None:
                zz = jnp.concatenate([jnp.where(never, after, zz[:, 0:LANES]), zz[:, LANES:]], axis=1)
            za = _gelu_from_half(zz)
            u_blocks[r0] = za[:, :A_WIDTH]
            vn = _ln(za[:, A_WIDTH:], alng_ref[...], alnb_ref[...])
            v_blocks[r0] = vn.astype(bf16)
            return _sample(za) if tie_pieces else None

        queue = [functools.partial(conv_piece, g, r0) for g in range(B_LANE_GROUPS) for r0 in row_blocks]
        queue += [functools.partial(conv_norm_piece, r0) for r0 in row_blocks]
        queue += [functools.partial(gelu_piece, r0) for r0 in row_blocks]
        queue = iter(queue)

    def tied(w, tiles, result=None):
        pieces = {}
        for i, tile in enumerate(tiles):
            piece = next(queue, None)
            if piece is not None:
                after = None
                if result is not None:
                    r0 = PIECE_ROWS * (i % len(row_blocks))
                    after = result[r0:r0 + PIECE_ROWS, 0:LANES]
                pieces[tile] = piece(after=after)
        return _tie(never, w, pieces) if pieces else w

    if do_channel:
        k_tiles = D_MODEL // MXU_TILE
        chunk_tiles = [(kt, 0) for kt in range(k_tiles)] + [(kt, 1) for kt in range(1, k_tiles, 2)]
        down_tiles = [(kt, nt) for kt in range(1, FFN_HIDDEN // MXU_TILE) for nt in range(0, D_MODEL // MXU_TILE, 2)]
        acts = []
        result = None
        for f0, f1 in _ffn_chunks():
            tiles = [tl for tl in chunk_tiles if (tl[1] + 1) * MXU_TILE <= f1 - f0]
            gate = jnp.dot(hn2, tied(w_gate_ref[:, f0:f1], tiles, result), preferred_element_type=f32)
            up = jnp.dot(hn2, tied(w_up_ref[:, f0:f1], tiles, result), preferred_element_type=f32)
            gb = gate.astype(bf16)
            acts.append(_one_plus_tanh(gb) * (gb * up.astype(bf16)))
            result = gate
        act = jnp.concatenate(acts, axis=1)
        ffn = jnp.dot(act, tied(w_down_ref[...], down_tiles, result), preferred_element_type=f32)
    for i, piece in enumerate(queue):
        piece()
        if after_piece is not None:
            after_piece(i)

    if do_mixer:
        nblk = TM // GMLP_BLOCK
        u = jnp.concatenate([u_blocks[r0] for r0 in row_blocks], axis=0)
        v = jnp.concatenate([v_blocks[r0] for r0 in row_blocks], axis=0)
        pos_i = lax.broadcasted_iota(jnp.int32, (GMLP_BLOCK, GMLP_BLOCK), 0) // CHUNK
        pos_j = lax.broadcasted_iota(jnp.int32, (GMLP_BLOCK, GMLP_BLOCK), 1) // CHUNK
        mask = pos_j <= pos_i
        for h in range(A_HEADS):
            w = jnp.where(mask, ws_ref[h], 0.0).astype(bf16)
            c0 = h * A_HEAD_DIM
            vh = jnp.concatenate(
                [v[n * GMLP_BLOCK:(n + 1) * GMLP_BLOCK, c0:c0 + A_HEAD_DIM] for n in range(nblk)],
                axis=1)
            sg = jnp.dot(w, vh, preferred_element_type=f32)
            bias = bs_ref[:, c0:c0 + A_HEAD_DIM]
            for n in range(nblk):
                r0 = n * GMLP_BLOCK
                ya = u[r0:r0 + GMLP_BLOCK, c0:c0 + A_HEAD_DIM] * (
                    sg[:, n * A_HEAD_DIM:(n + 1) * A_HEAD_DIM] + bias)
                y_ref[slot, r0:r0 + GMLP_BLOCK, c0:c0 + A_HEAD_DIM] = ya.astype(bf16)

    if do_channel:
        o_ref[...] = _rms(o_ref[...] + ffn, fng_ref[...])


def _block_kernel(tiles_per_seq, n_tiles, *refs):
    n_small = 11
    x_ref, xp_ref = refs[0:2]
    small = refs[2:2 + n_small]
    w_hbm = refs[2 + n_small:2 + n_small + len(WEIGHTS)]
    o_ref = refs[2 + n_small + len(WEIGHTS)]
    w_vmem = refs[3 + n_small + len(WEIGHTS):3 + n_small + 2 * len(WEIGHTS)]
    hbuf_ref, cbuf_ref, y_ref, stage_ref, sems = refs[3 + n_small + 2 * len(WEIGHTS):]
    slots = [stage_ref.at[i] for i in range(STAGE_SLOTS - 1)] + [o_ref]
    t = pl.program_id(0)
    args = (t, x_ref, xp_ref) + tuple(small) + tuple(w_vmem) + (o_ref, hbuf_ref, cbuf_ref, y_ref)

    @pl.when(t % tiles_per_seq == 0)
    def _():
        hbuf_ref[:, 0:HALO, :] = jnp.zeros((B_LANE_GROUPS, HALO, LANES), jnp.float32)

    loader = _WeightLoader(w_hbm, w_vmem, slots, sems)
    for step in range(LAG):
        @pl.when(t == step)
        def _(step=step):
            if step == 0:
                loader.prime()
                loader.service(loader.chunks_of(0))
            _step(True, False, *args,
                  after_piece=lambda i: loader.service(1) if i % PIECES_PER_CHUNK_LOAD == 0 else None)
            if step == LAG - 1:
                loader.service()

    @pl.when(jnp.logical_and(t >= LAG, t < n_tiles))
    def _():
        _step(True, True, *args)

    @pl.when(t >= n_tiles)
    def _():
        _step(False, True, *args)


def _const_spec(shape):
    nd = len(shape)
    return pl.BlockSpec(shape, lambda t: (0,) * nd, pipeline_mode=pl.Buffered(1))


def kernel(x, norm1_g, w_in, gmlp_ln_g, gmlp_ln_b, gmlp_w_s, gmlp_b_s, conv_w, conv_b,
           conv_ln_g, conv_ln_b, w_out, norm2_g, w_gate, w_up, w_down, final_norm_g):
    bsz, seq, d = x.shape
    assert d == D_MODEL and seq % TM == 0
    tiles_per_seq = seq // TM
    n_tiles = bsz * tiles_per_seq
    row = lambda a: a.reshape(1, -1)
    bs_tile = jnp.repeat(gmlp_b_s.T, A_HEAD_DIM, axis=1)
    x2 = x.reshape(bsz * seq, d)
    small = (row(norm1_g), row(gmlp_ln_g), row(gmlp_ln_b), gmlp_w_s, bs_tile, conv_w, row(conv_b),
             row(conv_ln_g), row(conv_ln_b), row(norm2_g), row(final_norm_g))
    weights = (w_in, w_out, w_gate, w_up, w_down)
    for w, (_, rows, cols, _) in zip(weights, WEIGHTS):
        assert w.shape == (rows, cols)
    cur_spec = pl.BlockSpec((TM, D_MODEL), lambda t: (jnp.minimum(t, n_tiles - 1), 0))
    prev_spec = pl.BlockSpec((TM, D_MODEL), lambda t: (jnp.maximum(t - LAG, 0), 0))
    in_specs = ([cur_spec, prev_spec] + [_const_spec(a.shape) for a in small]
                + [pl.BlockSpec(memory_space=pl.ANY)] * len(weights))
    out = pl.pallas_call(
        functools.partial(_block_kernel, tiles_per_seq, n_tiles),
        grid=(n_tiles + LAG,),
        in_specs=in_specs,
        out_specs=prev_spec,
        out_shape=jax.ShapeDtypeStruct(x2.shape, x.dtype),
        scratch_shapes=(
            [pltpu.VMEM((rows, cols), jnp.bfloat16) for _, rows, cols, _ in WEIGHTS] + [
                pltpu.VMEM((B_LANE_GROUPS, TM + HALO, LANES), jnp.float32),
                pltpu.VMEM((TM, B_WIDTH), jnp.float32),
                pltpu.VMEM((LAG, TM, D_MODEL), jnp.bfloat16),
                pltpu.VMEM((STAGE_SLOTS - 1, STAGE_ROWS, STAGE_COLS), jnp.float32),
                pltpu.SemaphoreType.DMA((STAGE_SLOTS,)),
            ]),
        compiler_params=pltpu.CompilerParams(
            dimension_semantics=("arbitrary",),
            vmem_limit_bytes=VMEM_LIMIT_BYTES),
        name="hybrid_block",
    )(x2, x2, *small, *weights)
    return out.reshape(x.shape)
```
